```python
import math
import jax, jax.numpy as jnp
from jax import lax
import numpy as np

D_MODEL = 2048
BATCH = 2
SEQ = 4096
DEPTH = 4
DEC_BATCH = 128
DEC_SEQ = 8
PAST_LEN = 8192
PAGE_SIZE = 128

N_MIXERS = 3
N_SSD = (DEPTH + 2) // 3
N_SB = (DEPTH + 1) // 3
N_MLA = DEPTH // 3
NORM_EPS = 1e-6
Q_BLOCK = 128

SSD_D_INNER = 2 * D_MODEL
SSD_HEAD_DIM = 64
SSD_HEADS = SSD_D_INNER // SSD_HEAD_DIM
SSD_GROUPS = 8
SSD_STATE = 128
SSD_CONV = 4
SSD_CHUNK = 128
SSD_CONV_DIM = SSD_D_INNER + 2 * SSD_GROUPS * SSD_STATE
SSD_IN = SSD_D_INNER + SSD_CONV_DIM + SSD_HEADS

SB_HEADS = 16
SB_KV_HEADS = 4
SB_GROUP = SB_HEADS // SB_KV_HEADS
SB_HEAD_DIM = 128
SB_WIDTH = SB_HEADS * SB_HEAD_DIM
SB_KV_WIDTH = SB_KV_HEADS * SB_HEAD_DIM
SB_IN = 2 * SB_WIDTH + 2 * SB_KV_WIDTH
SB_SCALE = SB_HEAD_DIM ** -0.5

MLA_HEADS = 16
MLA_Q_LORA = 512
MLA_KV_LORA = 256
MLA_NOPE = 128
MLA_ROPE = 64
MLA_V = 128
MLA_WIDTH = MLA_HEADS * MLA_V
MLA_IN = MLA_Q_LORA + MLA_KV_LORA + MLA_ROPE + MLA_WIDTH
MLA_SCALE = (MLA_NOPE + MLA_ROPE) ** -0.5
ROPE_THETA = 10000.0

kernel_name = "hybrid_ssd_stickbreak_mla_step"

F32 = jnp.float32


def rmsnorm(x, w):
    xf = x.astype(F32)
    y = xf * lax.rsqrt(jnp.mean(xf * xf, axis=-1, keepdims=True) + NORM_EPS)
    return (y * w.astype(F32)).astype(x.dtype)


def rope(x, pos):
    half = MLA_ROPE // 2
    inv_freq = ROPE_THETA ** (-jnp.arange(half, dtype=F32) / half)
    ang = pos.astype(F32)[:, None] * inv_freq[None, :]
    cos, sin = jnp.cos(ang)[:, None, :], jnp.sin(ang)[:, None, :]
    xf = x.astype(F32)
    x1, x2 = xf[..., :half], xf[..., half:]
    return jnp.concatenate([x1 * cos - x2 * sin, x2 * cos + x1 * sin], axis=-1).astype(x.dtype)


def causal_conv(xbc, buf, w, b):
    L = xbc.shape[1]
    full = jnp.concatenate([buf.astype(xbc.dtype), xbc], axis=1)
    acc = full[:, 0:L] * w[0]
    for kk in range(1, SSD_CONV):
        acc = acc + full[:, kk:kk + L] * w[kk]
    return jax.nn.silu(acc + b), full[:, L:]


def ssd_scan(x, dt, a, bm, cm, h0):
    b_, L = x.shape[:2]
    cs = SSD_CHUNK if L % SSD_CHUNK == 0 else L
    nc = L // cs
    hg = SSD_HEADS // SSD_GROUPS
    x = x.reshape(b_, nc, cs, SSD_GROUPS, hg, SSD_HEAD_DIM)
    dt = dt.reshape(b_, nc, cs, SSD_GROUPS, hg)
    bm = bm.reshape(b_, nc, cs, SSD_GROUPS, SSD_STATE)
    cm = cm.reshape(b_, nc, cs, SSD_GROUPS, SSD_STATE)
    la_cum = jnp.cumsum(dt * a.reshape(SSD_GROUPS, hg), axis=2)
    lc = jnp.moveaxis(la_cum, 2, -1)
    dtT = jnp.moveaxis(dt, 2, -1)
    causal = jnp.tril(jnp.ones((cs, cs), dtype=bool))
    seg = lc[..., :, None] - lc[..., None, :]
    decay = jnp.where(causal, jnp.exp(jnp.where(causal, seg, 0.0)), 0.0)
    cb = jnp.einsum("bctgn,bcsgn->bcgts", cm, bm)
    w = cb[:, :, :, None] * decay * dtT[..., None, :]
    y_diag = jnp.einsum("bcghts,bcsghp->bctghp", w, x)
    decay_end = jnp.exp(lc[..., -1:] - lc) * dtT
    states = jnp.einsum("bcsgn,bcghs,bcsghp->bcghpn", bm, decay_end, x)
    chunk_decay = jnp.exp(lc[..., -1])

    def step(h, inp):
        st, dec = inp
        return h * dec[..., None, None] + st, h

    hT, h_prev = lax.scan(step, h0.reshape(b_, SSD_GROUPS, hg, SSD_HEAD_DIM, SSD_STATE),
                          (jnp.moveaxis(states, 1, 0), jnp.moveaxis(chunk_decay, 1, 0)))
    h_prev = jnp.moveaxis(h_prev, 0, 1)
    y_off = jnp.einsum("bctgn,bcghpn,bcght->bctghp", cm, h_prev, jnp.exp(lc))
    y = (y_diag + y_off).reshape(b_, L, SSD_HEADS, SSD_HEAD_DIM)
    return y, hT.reshape(b_, SSD_HEADS, SSD_HEAD_DIM, SSD_STATE)


def ssd_mixer(u, conv_buf, h0, w_in, conv_w, conv_b, dt_bias, a_log, d_skip, norm_w, w_out):
    b_, L, _ = u.shape
    proj = u @ w_in
    z, xbc, dt = jnp.split(proj, [SSD_D_INNER, SSD_D_INNER + SSD_CONV_DIM], axis=-1)
    xbc_act, new_buf = causal_conv(xbc, conv_buf, conv_w, conv_b)
    xs, bm, cm = jnp.split(xbc_act, [SSD_D_INNER, SSD_D_INNER + SSD_GROUPS * SSD_STATE], axis=-1)
    xs = xs.astype(F32).reshape(b_, L, SSD_HEADS, SSD_HEAD_DIM)
    bm = bm.astype(F32).reshape(b_, L, SSD_GROUPS, SSD_STATE)
    cm = cm.astype(F32).reshape(b_, L, SSD_GROUPS, SSD_STATE)
    dt = jax.nn.softplus(dt.astype(F32) + dt_bias.astype(F32))
    a = -jnp.exp(a_log.astype(F32))
    y, hT = ssd_scan(xs, dt, a, bm, cm, h0.astype(F32))
    y = y + d_skip.astype(F32)[:, None] * xs
    y = y.reshape(b_, L, SSD_D_INNER) * jax.nn.silu(z.astype(F32))
    yg = y.reshape(b_, L, SSD_GROUPS, SSD_D_INNER // SSD_GROUPS)
    yg = yg * lax.rsqrt(jnp.mean(yg * yg, axis=-1, keepdims=True) + NORM_EPS)
    y = yg.reshape(b_, L, SSD_D_INNER) * norm_w.astype(F32)
    return y.astype(u.dtype) @ w_out, new_buf, hT.astype(u.dtype)


def sb_project(u, w_in):
    b_, L, _ = u.shape
    q, k, v, g = jnp.split(u @ w_in, [SB_WIDTH, SB_WIDTH + SB_KV_WIDTH, SB_WIDTH + 2 * SB_KV_WIDTH], axis=-1)
    q = q.reshape(b_, L, SB_KV_HEADS, SB_GROUP, SB_HEAD_DIM)
    k = k.reshape(b_, L, SB_KV_HEADS, SB_HEAD_DIM)
    v = v.reshape(b_, L, SB_KV_HEADS, SB_HEAD_DIM)
    return q, k, v, g


def sb_weights(q, k, valid, logsurv):
    z = jnp.einsum("bqhgd,bkhd->bhgqk", q, k, preferred_element_type=F32) * SB_SCALE
    l = jnp.where(valid, -jax.nn.softplus(z), 0.0)
    later = lax.cumsum(l, axis=l.ndim - 1, reverse=True) - l + logsurv[..., None]
    w = jnp.where(valid, jnp.exp(jax.nn.log_sigmoid(z) + later), 0.0)
    return w, logsurv + jnp.sum(l, axis=-1)


def sb_prompt_attend(q, k, v):
    b_, S = q.shape[:2]
    kpos = jnp.arange(S)
    logsurv0 = jnp.zeros((b_, SB_KV_HEADS, SB_GROUP, Q_BLOCK), F32)

    def block(i):
        start = i * Q_BLOCK
        qb = lax.dynamic_slice_in_dim(q, start, Q_BLOCK, axis=1)
        qpos = start + jnp.arange(Q_BLOCK)
        w, _ = sb_weights(qb, k, kpos[None, :] < qpos[:, None], logsurv0)
        return jnp.einsum("bhgqk,bkhd->bqhgd", w, v, preferred_element_type=F32)

    out = lax.map(block, jnp.arange(S // Q_BLOCK))
    return jnp.moveaxis(out, 0, 1).reshape(b_, S, SB_WIDTH)


def sb_sample_attend(q, k_new, v_new, cache_k, cache_v, j, page_table):
    db, T = q.shape[:2]
    tpos = jnp.arange(T)
    logsurv = jnp.zeros((db, SB_KV_HEADS, SB_GROUP, T), F32)
    w, logsurv = sb_weights(q, k_new, tpos[None, :] < tpos[:, None], logsurv)
    out = jnp.einsum("bhgqk,bkhd->bqhgd", w, v_new, preferred_element_type=F32)

    def page_step(carry, pages):
        ls, acc = carry
        w, ls = sb_weights(q, cache_k[j, pages], True, ls)
        acc = acc + jnp.einsum("bhgqk,bkhd->bqhgd", w, cache_v[j, pages], preferred_element_type=F32)
        return (ls, acc), None

    (_, out), _ = lax.scan(page_step, (logsurv, out), page_table[:, ::-1].T)
    return out.reshape(db, T, SB_WIDTH)


def gated_out(attn, g, w_out):
    return (attn * jax.nn.silu(g.astype(F32))).astype(g.dtype) @ w_out


def mla_project(u, pos, w_in, q_norm, kv_norm, w_uq, w_uk):
    o1, o2, o3 = MLA_Q_LORA, MLA_Q_LORA + MLA_KV_LORA, MLA_Q_LORA + MLA_KV_LORA + MLA_ROPE
    cq, ckv, kr, g = jnp.split(u @ w_in, [o1, o2, o3], axis=-1)
    cq = rmsnorm(cq, q_norm)
    ckv = rmsnorm(ckv, kv_norm)
    q = jnp.einsum("blr,rhd->blhd", cq, w_uq)
    q_rope = rope(q[..., MLA_NOPE:], pos)
    q_lat = jnp.einsum("blhd,rhd->blhr", q[..., :MLA_NOPE], w_uk)
    kr = rope(kr[:, :, None, :], pos)[:, :, 0, :]
    return q_lat, q_rope, ckv, kr, g


def mla_scores(q_lat, q_rope, ckv, kr):
    s = jnp.einsum("bqhr,bkr->bhqk", q_lat, ckv, preferred_element_type=F32)
    s = s + jnp.einsum("bqhd,bkd->bhqk", q_rope, kr, preferred_element_type=F32)
    return s * MLA_SCALE


def mla_prompt_attend(q_lat, q_rope, ckv, kr):
    b_, S = q_lat.shape[:2]
    kpos = jnp.arange(S)

    def block(i):
        start = i * Q_BLOCK
        ql = lax.dynamic_slice_in_dim(q_lat, start, Q_BLOCK, axis=1)
        qr = lax.dynamic_slice_in_dim(q_rope, start, Q_BLOCK, axis=1)
        qpos = start + jnp.arange(Q_BLOCK)
        s = jnp.where(kpos[None, :] <= qpos[:, None], mla_scores(ql, qr, ckv, kr), -jnp.inf)
        p = jax.nn.softmax(s, axis=-1)
        return jnp.einsum("bhqk,bkr->bqhr", p, ckv, preferred_element_type=F32)

    out = lax.map(block, jnp.arange(S // Q_BLOCK))
    return jnp.moveaxis(out, 0, 1).reshape(b_, S, MLA_HEADS, MLA_KV_LORA)


def mla_sample_attend(q_lat, q_rope, ckv_new, kr_new, cache_ckv, cache_kr, j, page_table):
    T = q_lat.shape[1]
    tpos = jnp.arange(T)
    s = jnp.where(tpos[None, :] <= tpos[:, None], mla_scores(q_lat, q_rope, ckv_new, kr_new), -jnp.inf)
    m = jnp.max(s, axis=-1)
    p = jnp.exp(s - m[..., None])
    l = jnp.sum(p, axis=-1)
    acc = jnp.einsum("bhqk,bkr->bhqr", p, ckv_new, preferred_element_type=F32)

    def page_step(carry, pages):
        m, l, acc = carry
        ckv_p = cache_ckv[j, pages]
        s = mla_scores(q_lat, q_rope, ckv_p, cache_kr[j, pages])
        m_new = jnp.maximum(m, jnp.max(s, axis=-1))
        corr = jnp.exp(m - m_new)
        p = jnp.exp(s - m_new[..., None])
        acc = acc * corr[..., None] + jnp.einsum("bhqk,bkr->bhqr", p, ckv_p, preferred_element_type=F32)
        return (m_new, l * corr + jnp.sum(p, axis=-1), acc), None

    (m, l, acc), _ = lax.scan(page_step, (m, l, acc), page_table.T)
    return jnp.swapaxes(acc / l[..., None], 1, 2)


def mla_output(o_lat, g, w_uv, w_out):
    b_, L = o_lat.shape[:2]
    o = jnp.einsum("blhr,rhd->blhd", o_lat, w_uv, preferred_element_type=F32).reshape(b_, L, MLA_WIDTH)
    return gated_out(o, g, w_out)


def setup_inputs(seed: int = 0) -> dict:
    key = jax.random.key(seed)
    ks = iter(jax.random.split(key, 40))

    def nrm(shape, scale):
        return jax.random.normal(next(ks), shape, F32) * scale

    n_pages = PAST_LEN // PAGE_SIZE
    n_pool = (DEC_BATCH * n_pages * 5) // 4
    page_table = jax.random.permutation(next(ks), n_pool)[: DEC_BATCH * n_pages]
    page_table = page_table.reshape(DEC_BATCH, n_pages).astype(jnp.int32)
    dt0 = jnp.exp(jax.random.uniform(next(ks), (N_SSD, SSD_HEADS), F32, math.log(1e-3), math.log(1e-1)))
    return {
        "x_prompt": nrm((BATCH, SEQ, D_MODEL), 1.0),
        "x_sample": nrm((DEC_BATCH, DEC_SEQ, D_MODEL), 1.0),
        "state_ssd_conv": nrm((N_SSD, DEC_BATCH, SSD_CONV - 1, SSD_CONV_DIM), 1.0),
        "state_ssd_h": nrm((N_SSD, DEC_BATCH, SSD_HEADS, SSD_HEAD_DIM, SSD_STATE), 0.1),
        "cache_sb_k": nrm((N_SB, n_pool, PAGE_SIZE, SB_KV_HEADS, SB_HEAD_DIM), 1.0),
        "cache_sb_v": nrm((N_SB, n_pool, PAGE_SIZE, SB_KV_HEADS, SB_HEAD_DIM), 1.0),
        "cache_mla_ckv": nrm((N_MLA, n_pool, PAGE_SIZE, MLA_KV_LORA), 1.0),
        "cache_mla_krope": nrm((N_MLA, n_pool, PAGE_SIZE, MLA_ROPE), 1.0),
        "page_table": page_table,
        "norm_pre": 1.0 + nrm((DEPTH, D_MODEL), 0.02),
        "norm_post": 1.0 + nrm((DEPTH, D_MODEL), 0.02),
        "ssd_w_in": nrm((N_SSD, D_MODEL, SSD_IN), D_MODEL ** -0.5),
        "ssd_conv_w": nrm((N_SSD, SSD_CONV, SSD_CONV_DIM), SSD_CONV ** -0.5),
        "ssd_conv_b": nrm((N_SSD, SSD_CONV_DIM), 0.02),
        "ssd_dt_bias": dt0 + jnp.log(-jnp.expm1(-dt0)),
        "ssd_a_log": jnp.log(jax.random.uniform(next(ks), (N_SSD, SSD_HEADS), F32, 1.0, 16.0)),
        "ssd_d": 1.0 + nrm((N_SSD, SSD_HEADS), 0.02),
        "ssd_norm_w": 1.0 + nrm((N_SSD, SSD_D_INNER), 0.02),
        "ssd_w_out": nrm((N_SSD, SSD_D_INNER, D_MODEL), SSD_D_INNER ** -0.5),
        "sb_w_in": nrm((N_SB, D_MODEL, SB_IN), D_MODEL ** -0.5),
        "sb_w_out": nrm((N_SB, SB_WIDTH, D_MODEL), SB_WIDTH ** -0.5),
        "mla_w_in": nrm((N_MLA, D_MODEL, MLA_IN), D_MODEL ** -0.5),
        "mla_q_norm": 1.0 + nrm((N_MLA, MLA_Q_LORA), 0.02),
        "mla_kv_norm": 1.0 + nrm((N_MLA, MLA_KV_LORA), 0.02),
        "mla_w_uq": nrm((N_MLA, MLA_Q_LORA, MLA_HEADS, MLA_NOPE + MLA_ROPE), MLA_Q_LORA ** -0.5),
        "mla_w_uk": nrm((N_MLA, MLA_KV_LORA, MLA_HEADS, MLA_NOPE), MLA_KV_LORA ** -0.5),
        "mla_w_uv": nrm((N_MLA, MLA_KV_LORA, MLA_HEADS, MLA_V), MLA_KV_LORA ** -0.5),
        "mla_w_out": nrm((N_MLA, MLA_WIDTH, D_MODEL), MLA_WIDTH ** -0.5),
    }


def reference(x_prompt, x_sample, state_ssd_conv, state_ssd_h, cache_sb_k, cache_sb_v, cache_mla_ckv,
              cache_mla_krope, page_table, norm_pre, norm_post, ssd_w_in, ssd_conv_w, ssd_conv_b,
              ssd_dt_bias, ssd_a_log, ssd_d, ssd_norm_w, ssd_w_out, sb_w_in, sb_w_out, mla_w_in,
              mla_q_norm, mla_kv_norm, mla_w_uq, mla_w_uk, mla_w_uv, mla_w_out):
    xp, xs = x_prompt, x_sample
    bp, S, _ = xp.shape
    T = xs.shape[1]
    past = page_table.shape[1] * cache_sb_k.shape[2]
    pos_p = jnp.arange(S)
    pos_s = past + jnp.arange(T)
    zero_buf = jnp.zeros((bp, SSD_CONV - 1, SSD_CONV_DIM), xp.dtype)
    zero_h = jnp.zeros((bp, SSD_HEADS, SSD_HEAD_DIM, SSD_STATE), F32)
    p_conv, p_h, p_k, p_v, p_ckv, p_kr = [], [], [], [], [], []
    s_conv, s_h, s_k, s_v, s_ckv, s_kr = [], [], [], [], [], []
    for i in range(DEPTH):
        kind, j = i % N_MIXERS, i // N_MIXERS
        hp = rmsnorm(xp, norm_pre[i])
        hs = rmsnorm(xs, norm_pre[i])
        if kind == 0:
            w = (ssd_w_in[j], ssd_conv_w[j], ssd_conv_b[j], ssd_dt_bias[j], ssd_a_log[j], ssd_d[j],
                 ssd_norm_w[j], ssd_w_out[j])
            op, cbuf, hT = ssd_mixer(hp, zero_buf, zero_h, *w)
            os_, cbuf_s, hT_s = ssd_mixer(hs, state_ssd_conv[j], state_ssd_h[j], *w)
            p_conv.append(cbuf); p_h.append(hT); s_conv.append(cbuf_s); s_h.append(hT_s)
        elif kind == 1:
            q, k, v, g = sb_project(hp, sb_w_in[j])
            op = gated_out(sb_prompt_attend(q, k, v), g, sb_w_out[j])
            p_k.append(k); p_v.append(v)
            q, k, v, g = sb_project(hs, sb_w_in[j])
            os_ = gated_out(sb_sample_attend(q, k, v, cache_sb_k, cache_sb_v, j, page_table), g, sb_w_out[j])
            s_k.append(k); s_v.append(v)
        else:
            q_lat, q_rope, ckv, kr, g = mla_project(hp, pos_p, mla_w_in[j], mla_q_norm[j], mla_kv_norm[j],
                                                    mla_w_uq[j], mla_w_uk[j])
            op = mla_output(mla_prompt_attend(q_lat, q_rope, ckv, kr), g, mla_w_uv[j], mla_w_out[j])
            p_ckv.append(ckv); p_kr.append(kr)
            q_lat, q_rope, ckv, kr, g = mla_project(hs, pos_s, mla_w_in[j], mla_q_norm[j], mla_kv_norm[j],
                                                    mla_w_uq[j], mla_w_uk[j])
            o_lat = mla_sample_attend(q_lat, q_rope, ckv, kr, cache_mla_ckv, cache_mla_krope, j, page_table)
            os_ = mla_output(o_lat, g, mla_w_uv[j], mla_w_out[j])
            s_ckv.append(ckv); s_kr.append(kr)
        xp = xp + rmsnorm(op, norm_post[i])
        xs = xs + rmsnorm(os_, norm_post[i])
    return (xp, xs,
            jnp.stack(p_conv), jnp.stack(p_h), jnp.stack(p_k), jnp.stack(p_v), jnp.stack(p_ckv), jnp.stack(p_kr),
            jnp.stack(s_conv), jnp.stack(s_h), jnp.stack(s_k), jnp.stack(s_v), jnp.stack(s_ckv), jnp.stack(s_kr))
```

```python
import functools
import math

import numpy as np
import jax
import jax.numpy as jnp
from jax import lax
from jax.experimental import pallas as pl
from jax.experimental.pallas import tpu as pltpu

F32 = jnp.float32
BF16 = jnp.bfloat16

D_MODEL = 2048
N_MIXERS = 3
NORM_EPS = 1e-6

SSD_D_INNER = 2 * D_MODEL
SSD_HEAD_DIM = 64
SSD_HEADS = SSD_D_INNER // SSD_HEAD_DIM
SSD_GROUPS = 8
SSD_STATE = 128
SSD_CONV = 4
SSD_CHUNK = 128
SSD_BC = SSD_GROUPS * SSD_STATE
SSD_CONV_DIM = SSD_D_INNER + 2 * SSD_BC
SSD_GROUP_WIDTH = SSD_D_INNER // SSD_GROUPS
SSD_HEADS_PER_GROUP = SSD_HEADS // SSD_GROUPS

SB_HEADS = 16
SB_KV_HEADS = 4
SB_GROUP = SB_HEADS // SB_KV_HEADS
SB_HEAD_DIM = 128
SB_WIDTH = SB_HEADS * SB_HEAD_DIM
SB_KV_WIDTH = SB_KV_HEADS * SB_HEAD_DIM
SB_SCALE = SB_HEAD_DIM ** -0.5

MLA_HEADS = 16
MLA_Q_LORA = 512
MLA_KV_LORA = 256
MLA_NOPE = 128
MLA_ROPE = 64
MLA_V = 128
MLA_WIDTH = MLA_HEADS * MLA_V
MLA_SCALE = (MLA_NOPE + MLA_ROPE) ** -0.5
ROPE_THETA = 10000.0

LANES = 128
SUBLANES = 8
VMEM_LIMIT_BYTES = 48 * 1024 * 1024
NEG_BIG = -1e30


def _cparams(sem):
    return pltpu.CompilerParams(dimension_semantics=sem, vmem_limit_bytes=VMEM_LIMIT_BYTES)


def _dot(a, b):
    return jnp.dot(a, b, preferred_element_type=F32)


def _dot_nt(a, b):
    return lax.dot_general(a, b, (((1,), (1,)), ((), ())), preferred_element_type=F32)


def _dot_tn(a, b):
    return lax.dot_general(a, b, (((0,), (0,)), ((), ())), preferred_element_type=F32)


def _split_bf16(a, terms):
    parts, r = [], a
    for i in range(terms):
        p = r.astype(BF16)
        parts.append(p)
        if i + 1 < terms:
            r = r - p.astype(F32)
    return parts


def _dot_sel_rhs(a, sel, terms):
    out = None
    for p in _split_bf16(a, terms):
        d = _dot(p, sel)
        out = d if out is None else out + d
    return out


def _dot_sel_lhs(sel, a, terms):
    out = None
    for p in _split_bf16(a, terms):
        d = _dot(sel, p)
        out = d if out is None else out + d
    return out


def _row_mod(shape, n):
    assert n & (n - 1) == 0
    return lax.broadcasted_iota(jnp.int32, shape, 0) & (n - 1)


def _softplus(x):
    return jnp.maximum(x, 0.0) + jnp.log1p(jnp.exp(-jnp.abs(x)))


def _silu(x):
    return x * jax.nn.sigmoid(x)


def _norm_matmul_kernel(x_ref, nw_ref, w_ref, o_ref, xn_ref):
    @pl.when(pl.program_id(1) == 0)
    def _():
        x = x_ref[...]
        ms = jnp.mean(x * x, axis=-1, keepdims=True)
        xn_ref[...] = (x * lax.rsqrt(ms + NORM_EPS) * nw_ref[...]).astype(BF16)

    o_ref[...] = _dot(xn_ref[...], w_ref[...])


def _norm_matmul(x, nw, w, *, col_block=0, name):
    m = x.shape[0]
    k, n = w.shape
    tm = min(m, 1024)
    tn = min(n, 512)
    return pl.pallas_call(
        _norm_matmul_kernel,
        grid=(m // tm, n // tn),
        in_specs=[
            pl.BlockSpec((tm, k), lambda i, j: (i, col_block)),
            pl.BlockSpec((1, k), lambda i, j: (0, 0)),
            pl.BlockSpec((k, tn), lambda i, j: (0, j)),
        ],
        out_specs=pl.BlockSpec((tm, tn), lambda i, j: (i, j)),
        out_shape=jax.ShapeDtypeStruct((m, n), F32),
        scratch_shapes=[pltpu.VMEM((tm, k), BF16)],
        compiler_params=_cparams(("parallel", "arbitrary")),
        name=name,
    )(x, nw.reshape(1, k), w)


def _post_kernel(*refs, gated, nk):
    if gated:
        a_ref, g_ref, w_ref, res_ref, pw_ref, o_ref, acc_ref = refs
    else:
        a_ref, w_ref, res_ref, pw_ref, o_ref, acc_ref = refs
    kk = pl.program_id(1)
    a = a_ref[...].astype(F32)
    if gated:
        a = a * _silu(g_ref[...])
    part = _dot(a.astype(BF16), w_ref[...])

    @pl.when(kk == 0)
    def _():
        acc_ref[...] = part

    @pl.when(kk > 0)
    def _():
        acc_ref[...] += part

    @pl.when(kk == nk - 1)
    def _():
        y = acc_ref[...]
        ms = jnp.mean(y * y, axis=-1, keepdims=True)
        o_ref[...] = res_ref[...] + y * lax.rsqrt(ms + NORM_EPS) * pw_ref[...]


def _matmul_post(a, w, res, pw, *, gate=None, gate_col=0, name):
    m, k = a.shape
    n = w.shape[1]
    tm = min(m, 512)
    tk = min(k, 512)
    nk = k // tk
    gated = gate is not None
    in_specs = [pl.BlockSpec((tm, tk), lambda i, kk: (i, kk))]
    args = [a]
    if gated:
        gb = gate_col // tk
        in_specs.append(pl.BlockSpec((tm, tk), lambda i, kk: (i, gb + kk)))
        args.append(gate)
    in_specs += [
        pl.BlockSpec((tk, n), lambda i, kk: (kk, 0)),
        pl.BlockSpec((tm, n), lambda i, kk: (i, 0)),
        pl.BlockSpec((1, n), lambda i, kk: (0, 0)),
    ]
    args += [w, res, pw.reshape(1, n)]
    return pl.pallas_call(
        functools.partial(_post_kernel, gated=gated, nk=nk),
        grid=(m // tm, nk),
        in_specs=in_specs,
        out_specs=pl.BlockSpec((tm, n), lambda i, kk: (i, 0)),
        out_shape=jax.ShapeDtypeStruct((m, n), F32),
        scratch_shapes=[pltpu.VMEM((tm, n), F32)],
        compiler_params=_cparams(("parallel", "arbitrary")),
        name=name,
    )(*args)


def _conv_taps(ext, cur, w, bias, rows):
    off = SUBLANES - (SSD_CONV - 1)
    acc = ext[off:off + rows] * w[0:1]
    for k in range(1, SSD_CONV - 1):
        acc = acc + ext[off + k:off + k + rows] * w[k:k + 1]
    acc = acc + cur * w[SSD_CONV - 1:SSD_CONV]
    return _silu(acc + bias)


def _gate_norm(y, z, nw):
    y = y * _silu(z)
    ms = jnp.mean(y * y, axis=-1, keepdims=True)
    return y * lax.rsqrt(ms + NORM_EPS) * nw


def _ssd_prompt_kernel(z_ref, xs_ref, bm_ref, cm_ref, dt_ref, cw_ref, cb_ref, dtb_ref, alog_ref,
                       dsk_ref, nw_ref, y_ref, h_ref, tail_ref):
    cs = SSD_CHUNK

    @pl.when(pl.program_id(1) == 0)
    def _():
        h_ref[...] = jnp.zeros_like(h_ref)
        tail_ref[...] = jnp.zeros_like(tail_ref)

    def conv(cur, lo, width):
        ext = jnp.concatenate([tail_ref[:, lo:lo + width], cur], axis=0)
        out = _conv_taps(ext, cur, cw_ref[:, lo:lo + width], cb_ref[:, lo:lo + width], cs)
        tail_ref[:, lo:lo + width] = cur[cs - SUBLANES:cs]
        return out

    row = lax.broadcasted_iota(jnp.int32, (cs, cs), 0)
    col = lax.broadcasted_iota(jnp.int32, (cs, cs), 1)
    causal = row >= col
    tril = causal.astype(BF16)
    left = col < SSD_HEAD_DIM

    dt = _softplus(dt_ref[...] + dtb_ref[...])
    la = dt * (-jnp.exp(alog_ref[...]))
    lc = _dot_sel_lhs(tril, la, 3)
    lc_t = lc.T
    dt_t = dt.T
    e_all = jnp.exp(lc)
    lc_last = lc[cs - 1:cs, :]
    de_all = jnp.exp(lc_last - lc) * dt
    cd_all = jnp.exp(lc_last)

    b_act = conv(bm_ref[...], SSD_D_INNER, SSD_BC)
    c_act = conv(cm_ref[...], SSD_D_INNER + SSD_BC, SSD_BC)

    for g in range(SSD_GROUPS):
        gl = g * SSD_GROUP_WIDTH
        xg = conv(xs_ref[:, gl:gl + SSD_GROUP_WIDTH], gl, SSD_GROUP_WIDTH)
        bg = b_act[:, g * SSD_STATE:(g + 1) * SSD_STATE].astype(BF16)
        cg = c_act[:, g * SSD_STATE:(g + 1) * SSD_STATE].astype(BF16)
        cbm = _dot_nt(cg, bg)
        h_prev = h_ref[0, gl:gl + SSD_GROUP_WIDTH, :]
        y_off = _dot_nt(cg, h_prev.astype(BF16))
        y_parts, xw_parts, cd_parts = [], [], []
        for pr in range(SSD_HEADS_PER_GROUP // 2):
            xp = xg[:, pr * LANES:(pr + 1) * LANES]
            ws, es, des = [], [], []
            for hh in (0, 1):
                h = g * SSD_HEADS_PER_GROUP + 2 * pr + hh
                seg = lc[:, h:h + 1] - lc_t[h:h + 1, :]
                decay = jnp.where(causal, jnp.exp(jnp.where(causal, seg, 0.0)), 0.0)
                ws.append(cbm * decay * dt_t[h:h + 1, :])
                es.append(jnp.broadcast_to(e_all[:, h:h + 1], (cs, LANES)))
                des.append(jnp.broadcast_to(de_all[:, h:h + 1], (cs, LANES)))
                cd_parts.append(jnp.broadcast_to(cd_all[:, h:h + 1], (SSD_HEAD_DIM, SSD_STATE)))
            w2 = jnp.concatenate(ws, axis=1).astype(BF16)
            x2 = jnp.concatenate([jnp.where(left, xp, 0.0), jnp.where(left, 0.0, xp)],
                                 axis=0).astype(BF16)
            y_diag = _dot(w2, x2)
            e2 = jnp.where(left, es[0], es[1])
            y_parts.append(y_diag + y_off[:, pr * LANES:(pr + 1) * LANES] * e2
                           + dsk_ref[:, gl + pr * LANES:gl + (pr + 1) * LANES] * xp)
            xw_parts.append(xp * jnp.where(left, des[0], des[1]))
        yg = jnp.concatenate(y_parts, axis=1)
        y_ref[:, gl:gl + SSD_GROUP_WIDTH] = _gate_norm(
            yg, z_ref[:, gl:gl + SSD_GROUP_WIDTH], nw_ref[:, gl:gl + SSD_GROUP_WIDTH]).astype(y_ref.dtype)
        xw = jnp.concatenate(xw_parts, axis=1).astype(BF16)
        states = _dot_tn(xw, bg)
        h_ref[0, gl:gl + SSD_GROUP_WIDTH, :] = h_prev * jnp.concatenate(cd_parts, axis=0) + states


def _ssd_consts(conv_w, conv_b, dt_bias, a_log, d_skip, norm_w):
    pad = LANES - SSD_HEADS
    return (conv_w, conv_b.reshape(1, SSD_CONV_DIM),
            jnp.pad(dt_bias, (0, pad)).reshape(1, LANES), jnp.pad(a_log, (0, pad)).reshape(1, LANES),
            jnp.repeat(d_skip, SSD_HEAD_DIM).reshape(1, SSD_D_INNER), norm_w.reshape(1, SSD_D_INNER))


def _ssd_prompt(proj, dtraw, consts, batch, seq):
    cs = SSD_CHUNK
    nc = seq // cs
    zb = SSD_D_INNER // SSD_BC
    const_specs = [
        pl.BlockSpec((SSD_CONV, SSD_CONV_DIM), lambda b, c: (0, 0)),
        pl.BlockSpec((1, SSD_CONV_DIM), lambda b, c: (0, 0)),
        pl.BlockSpec((1, LANES), lambda b, c: (0, 0)),
        pl.BlockSpec((1, LANES), lambda b, c: (0, 0)),
        pl.BlockSpec((1, SSD_D_INNER), lambda b, c: (0, 0)),
        pl.BlockSpec((1, SSD_D_INNER), lambda b, c: (0, 0)),
    ]
    return pl.pallas_call(
        _ssd_prompt_kernel,
        grid=(batch, nc),
        in_specs=[
            pl.BlockSpec((cs, SSD_D_INNER), lambda b, c: (b * nc + c, 0)),
            pl.BlockSpec((cs, SSD_D_INNER), lambda b, c: (b * nc + c, 1)),
            pl.BlockSpec((cs, SSD_BC), lambda b, c: (b * nc + c, 2 * zb)),
            pl.BlockSpec((cs, SSD_BC), lambda b, c: (b * nc + c, 2 * zb + 1)),
            pl.BlockSpec((cs, LANES), lambda b, c: (b * nc + c, 0)),
        ] + const_specs,
        out_specs=[
            pl.BlockSpec((cs, SSD_D_INNER), lambda b, c: (b * nc + c, 0)),
            pl.BlockSpec((1, SSD_D_INNER, SSD_STATE), lambda b, c: (b, 0, 0)),
        ],
        out_shape=[
            jax.ShapeDtypeStruct((batch * seq, SSD_D_INNER), BF16),
            jax.ShapeDtypeStruct((batch, SSD_D_INNER, SSD_STATE), F32),
        ],
        scratch_shapes=[pltpu.VMEM((SUBLANES, SSD_CONV_DIM), F32)],
        compiler_params=_cparams(("parallel", "arbitrary")),
        name="ssd_prompt",
    )(proj, proj, proj, proj, dtraw, *consts)


def _ssd_sample_kernel(z_ref, xs_ref, bm_ref, cm_ref, dt_ref, st_ref, h0_ref, cw_ref, cb_ref, dtb_ref,
                       alog_ref, dsk_ref, nw_ref, rexp_ref, gsel_ref, y_ref, h_ref):
    t = xs_ref.shape[0]
    pairs = t * t

    def conv(cur, lo, width):
        ext = jnp.concatenate([st_ref[0, :, lo:lo + width], cur], axis=0)
        return _conv_taps(ext, cur, cw_ref[:, lo:lo + width], cb_ref[:, lo:lo + width], t)

    def tile_t(a):
        return jnp.concatenate([a] * t, axis=0)

    def tile_s(a):
        return jnp.concatenate([jnp.broadcast_to(a[s:s + 1], a.shape) for s in range(t)], axis=0)

    x_act = conv(xs_ref[...], 0, SSD_D_INNER)
    b_act = conv(bm_ref[...], SSD_D_INNER, SSD_BC)
    c_act = conv(cm_ref[...], SSD_D_INNER + SSD_BC, SSD_BC)

    dt = _softplus(dt_ref[...] + dtb_ref[...])
    la = dt * (-jnp.exp(alog_ref[...]))
    rix = lax.broadcasted_iota(jnp.int32, (t, LANES), 0)
    lc = la
    sh = 1
    while sh < t:
        lc = lc + jnp.where(rix >= sh, pltpu.roll(lc, sh, 0), 0.0)
        sh *= 2
    lc_last = lc[t - 1:t, :]
    rexp = rexp_ref[...]

    r = lax.broadcasted_iota(jnp.int32, (pairs, LANES), 0)
    ok = lax.shift_right_logical(r, int(math.log2(t))) <= (r & (t - 1))
    decay = jnp.where(ok, jnp.exp(jnp.where(ok, tile_t(lc) - tile_s(lc), 0.0)), 0.0)
    cbe = _dot_sel_rhs(tile_t(c_act) * tile_s(b_act), gsel_ref[...], 3)
    coef = _dot_sel_rhs(cbe * decay * tile_s(dt), rexp, 3)
    y = dsk_ref[...] * x_act
    for s in range(t):
        y = y + coef[s * t:(s + 1) * t] * jnp.broadcast_to(x_act[s:s + 1], x_act.shape)

    e_exp = _dot_sel_rhs(jnp.exp(lc), rexp, 3)
    xw = x_act * _dot_sel_rhs(jnp.exp(lc_last - lc) * dt, rexp, 3)
    cd_all = jnp.exp(lc_last)
    zero_rows = jnp.zeros((LANES - t, SSD_GROUP_WIDTH), F32)
    zero_b = jnp.zeros((LANES - t, SSD_STATE), F32)
    for g in range(SSD_GROUPS):
        gl = g * SSD_GROUP_WIDTH
        bg = b_act[:, g * SSD_STATE:(g + 1) * SSD_STATE]
        cg = c_act[:, g * SSD_STATE:(g + 1) * SSD_STATE]
        h_prev = h0_ref[0, gl:gl + SSD_GROUP_WIDTH, :]
        cg16 = jnp.concatenate([cg, jnp.zeros_like(cg)], axis=0).astype(BF16)
        y_off = _dot_nt(cg16, h_prev.astype(BF16))[:t]
        yg = y[:, gl:gl + SSD_GROUP_WIDTH] + y_off * e_exp[:, gl:gl + SSD_GROUP_WIDTH]
        y_ref[:, gl:gl + SSD_GROUP_WIDTH] = _gate_norm(
            yg, z_ref[:, gl:gl + SSD_GROUP_WIDTH], nw_ref[:, gl:gl + SSD_GROUP_WIDTH]).astype(y_ref.dtype)
        xw_pad = jnp.concatenate([xw[:, gl:gl + SSD_GROUP_WIDTH], zero_rows], axis=0).astype(BF16)
        b_pad = jnp.concatenate([bg, zero_b], axis=0).astype(BF16)
        states = _dot_tn(xw_pad, b_pad)
        cd = jnp.concatenate(
            [jnp.broadcast_to(cd_all[:, h:h + 1], (SSD_HEAD_DIM, SSD_STATE))
             for h in range(g * SSD_HEADS_PER_GROUP, (g + 1) * SSD_HEADS_PER_GROUP)], axis=0)
        h_ref[0, gl:gl + SSD_GROUP_WIDTH, :] = h_prev * cd + states


def _ssd_sample(proj, dtraw, conv_state, h0, consts, batch, t):
    assert t == SUBLANES, "sample SSD path handles one sublane tile of new tokens"
    zb = SSD_D_INNER // SSD_BC
    hist = jnp.pad(conv_state, ((0, 0), (SUBLANES - (SSD_CONV - 1), 0), (0, 0)))
    heads = np.arange(LANES)
    rexp = (heads[:, None] == (np.arange(SSD_D_INNER) // SSD_HEAD_DIM)[None, :])
    gsel = ((np.arange(SSD_BC) // SSD_STATE)[:, None] == (heads // SSD_HEADS_PER_GROUP)[None, :])
    gsel = gsel & (heads < SSD_HEADS)[None, :]
    rexp = jnp.asarray(rexp, BF16)
    gsel = jnp.asarray(gsel, BF16)
    full = lambda shape: pl.BlockSpec(shape, lambda b: (0,) * len(shape))
    return pl.pallas_call(
        _ssd_sample_kernel,
        grid=(batch,),
        in_specs=[
            pl.BlockSpec((t, SSD_D_INNER), lambda b: (b, 0)),
            pl.BlockSpec((t, SSD_D_INNER), lambda b: (b, 1)),
            pl.BlockSpec((t, SSD_BC), lambda b: (b, 2 * zb)),
            pl.BlockSpec((t, SSD_BC), lambda b: (b, 2 * zb + 1)),
            pl.BlockSpec((t, LANES), lambda b: (b, 0)),
            pl.BlockSpec((1, SUBLANES, SSD_CONV_DIM), lambda b: (b, 0, 0)),
            pl.BlockSpec((1, SSD_D_INNER, SSD_STATE), lambda b: (b, 0, 0)),
            full((SSD_CONV, SSD_CONV_DIM)), full((1, SSD_CONV_DIM)), full((1, LANES)), full((1, LANES)),
            full((1, SSD_D_INNER)), full((1, SSD_D_INNER)),
            full((LANES, SSD_D_INNER)), full((SSD_BC, LANES)),
        ],
        out_specs=[
            pl.BlockSpec((t, SSD_D_INNER), lambda b: (b, 0)),
            pl.BlockSpec((1, SSD_D_INNER, SSD_STATE), lambda b: (b, 0, 0)),
        ],
        out_shape=[
            jax.ShapeDtypeStruct((batch * t, SSD_D_INNER), F32),
            jax.ShapeDtypeStruct((batch, SSD_D_INNER, SSD_STATE), F32),
        ],
        compiler_params=_cparams(("parallel",)),
        name="ssd_sample",
    )(proj, proj, proj, proj, dtraw, hist, h0, *consts, rexp, gsel)


def _causal_steps(seq, tq, tk, descending):
    qi, kj, first, last = [], [], [], []
    for i in range(seq // tq):
        n = (i * tq + tq - 1) // tk + 1
        order = range(n - 1, -1, -1) if descending else range(n)
        for a, j in enumerate(order):
            qi.append(i), kj.append(j), first.append(int(a == 0)), last.append(int(a == n - 1))
    return tuple(jnp.asarray(np.asarray(v, np.int32)) for v in (qi, kj, first, last))


def _sb_block(zb, valid, vb, ls, usel):
    sp = _softplus(zb)
    l = -sp if valid is None else jnp.where(valid, -sp, 0.0)
    later = _dot_sel_rhs(l, usel, 2) + ls
    w = jnp.exp(zb - sp + later)
    if valid is not None:
        w = jnp.where(valid, w, 0.0)
    return _dot(w.astype(BF16), vb), ls + jnp.sum(l, axis=-1, keepdims=True)


def _suffix_sel():
    a = lax.broadcasted_iota(jnp.int32, (LANES, LANES), 0)
    b = lax.broadcasted_iota(jnp.int32, (LANES, LANES), 1)
    return (a > b).astype(BF16)


def _sb_prompt_kernel(qi_ref, kj_ref, first_ref, last_ref, q_ref, k_ref, v_ref, o_ref,
                      qs_ref, ls_ref, acc_ref, *, tq, tk):
    s = pl.program_id(2)
    rows = SB_GROUP * tq

    @pl.when(first_ref[s] == 1)
    def _():
        qs_ref[...] = jnp.concatenate(
            [q_ref[:, g * SB_HEAD_DIM:(g + 1) * SB_HEAD_DIM] for g in range(SB_GROUP)], axis=0).astype(BF16)
        ls_ref[...] = jnp.zeros_like(ls_ref)
        acc_ref[...] = jnp.zeros_like(acc_ref)

    kb = k_ref[...].astype(BF16)
    vb = v_ref[...].astype(BF16)
    z = _dot_nt(qs_ref[...], kb) * SB_SCALE
    qpos = qi_ref[s] * tq + _row_mod((rows, LANES), tq)
    kpos0 = kj_ref[s] * tk + lax.broadcasted_iota(jnp.int32, (rows, LANES), 1)
    usel = _suffix_sel()
    ls = ls_ref[...]
    acc = acc_ref[...]
    for blk in range(tk // LANES - 1, -1, -1):
        valid = (kpos0 + blk * LANES) < qpos
        pv, ls = _sb_block(z[:, blk * LANES:(blk + 1) * LANES], valid, vb[blk * LANES:(blk + 1) * LANES], ls, usel)
        acc = acc + pv
    ls_ref[...] = ls
    acc_ref[...] = acc

    @pl.when(last_ref[s] == 1)
    def _():
        for g in range(SB_GROUP):
            o_ref[:, g * SB_HEAD_DIM:(g + 1) * SB_HEAD_DIM] = acc[g * tq:(g + 1) * tq]


def _sb_prompt(proj, batch, seq):
    tq = min(seq, 256)
    tk = tq
    tabs = _causal_steps(seq, tq, tk, descending=True)
    nsteps = tabs[0].shape[0]
    nq, nkb = seq // tq, seq // tk
    kcol = SB_WIDTH // SB_HEAD_DIM
    vcol = (SB_WIDTH + SB_KV_WIDTH) // SB_HEAD_DIM
    rows = SB_GROUP * tq
    grid_spec = pltpu.PrefetchScalarGridSpec(
        num_scalar_prefetch=4,
        grid=(batch, SB_KV_HEADS, nsteps),
        in_specs=[
            pl.BlockSpec((tq, SB_GROUP * SB_HEAD_DIM), lambda b, h, s, qi, kj, f, l: (b * nq + qi[s], h)),
            pl.BlockSpec((tk, SB_HEAD_DIM), lambda b, h, s, qi, kj, f, l: (b * nkb + kj[s], kcol + h)),
            pl.BlockSpec((tk, SB_HEAD_DIM), lambda b, h, s, qi, kj, f, l: (b * nkb + kj[s], vcol + h)),
        ],
        out_specs=pl.BlockSpec((tq, SB_GROUP * SB_HEAD_DIM), lambda b, h, s, qi, kj, f, l: (b * nq + qi[s], h)),
        scratch_shapes=[pltpu.VMEM((rows, SB_HEAD_DIM), BF16), pltpu.VMEM((rows, 1), F32),
                        pltpu.VMEM((rows, SB_HEAD_DIM), F32)],
    )
    return pl.pallas_call(
        functools.partial(_sb_prompt_kernel, tq=tq, tk=tk),
        grid_spec=grid_spec,
        out_shape=jax.ShapeDtypeStruct((batch * seq, SB_WIDTH), F32),
        compiler_params=_cparams(("parallel", "parallel", "arbitrary")),
        name="sb_prompt",
    )(*tabs, proj, proj, proj)


def _sb_sample_kernel(pt_ref, q_ref, kn_ref, vn_ref, *refs, t, npp):
    k_refs, v_refs = refs[:npp], refs[npp:2 * npp]
    o_ref, ls_ref, acc_ref = refs[2 * npp:]
    c = pl.program_id(1)
    rows = SB_HEADS * t
    usel = _suffix_sel()
    qs = jnp.concatenate([q_ref[:, h * SB_HEAD_DIM:(h + 1) * SB_HEAD_DIM] for h in range(SB_HEADS)],
                         axis=0).astype(BF16)
    grows = SB_GROUP * t

    def page(kp, vp, valid, ls, acc):
        kp = kp.astype(BF16)
        vp = vp.astype(BF16)
        z = jnp.concatenate(
            [_dot_nt(qs[h * grows:(h + 1) * grows], kp[:, h * SB_HEAD_DIM:(h + 1) * SB_HEAD_DIM])
             for h in range(SB_KV_HEADS)], axis=0) * SB_SCALE
        sp = _softplus(z)
        l = -sp if valid is None else jnp.where(valid, -sp, 0.0)
        later = _dot_sel_rhs(l, usel, 2) + ls
        w = jnp.exp(z - sp + later)
        if valid is not None:
            w = jnp.where(valid, w, 0.0)
        w = w.astype(BF16)
        pv = jnp.concatenate(
            [_dot(w[h * grows:(h + 1) * grows], vp[:, h * SB_HEAD_DIM:(h + 1) * SB_HEAD_DIM])
             for h in range(SB_KV_HEADS)], axis=0)
        return ls + jnp.sum(l, axis=-1, keepdims=True), acc + pv

    @pl.when(c == 0)
    def _():
        pad = jnp.zeros((LANES - t, SB_KV_WIDTH), F32)
        kn = jnp.concatenate([kn_ref[...], pad], axis=0)
        vn = jnp.concatenate([vn_ref[...], pad], axis=0)
        tq = _row_mod((rows, LANES), t)
        kidx = lax.broadcasted_iota(jnp.int32, (rows, LANES), 1)
        ls, acc = page(kn, vn, kidx < tq, jnp.zeros((rows, 1), F32), jnp.zeros((rows, SB_HEAD_DIM), F32))
        ls_ref[...] = ls
        acc_ref[...] = acc

    ls = ls_ref[...]
    acc = acc_ref[...]
    for p in range(npp):
        ls, acc = page(k_refs[p][0], v_refs[p][0], None, ls, acc)
    ls_ref[...] = ls
    acc_ref[...] = acc

    @pl.when(c == pl.num_programs(1) - 1)
    def _():
        for h in range(SB_HEADS):
            o_ref[:, h * SB_HEAD_DIM:(h + 1) * SB_HEAD_DIM] = acc[h * t:(h + 1) * t]


def _pages_per_step(n_pages):
    return max(p for p in range(1, 9) if n_pages % p == 0)


def _sb_sample(proj, cache_k, cache_v, j, page_table, batch, t):
    n_pool, page = cache_k.shape[1], cache_k.shape[2]
    assert page == LANES
    n_pages = page_table.shape[1]
    npp = _pages_per_step(n_pages)
    ck = cache_k.reshape(-1, page, SB_KV_WIDTH)
    cv = cache_v.reshape(-1, page, SB_KV_WIDTH)
    pt = (page_table + j * n_pool).reshape(-1)
    kcol = SB_WIDTH // SB_KV_WIDTH

    def page_spec(p):
        return pl.BlockSpec((1, page, SB_KV_WIDTH),
                            lambda b, c, pt: (pt[b * n_pages + n_pages - 1 - (c * npp + p)], 0, 0))

    grid_spec = pltpu.PrefetchScalarGridSpec(
        num_scalar_prefetch=1,
        grid=(batch, n_pages // npp),
        in_specs=[
            pl.BlockSpec((t, SB_WIDTH), lambda b, c, pt: (b, 0)),
            pl.BlockSpec((t, SB_KV_WIDTH), lambda b, c, pt: (b, kcol)),
            pl.BlockSpec((t, SB_KV_WIDTH), lambda b, c, pt: (b, kcol + 1)),
        ] + [page_spec(p) for p in range(npp)] * 2,
        out_specs=pl.BlockSpec((t, SB_WIDTH), lambda b, c, pt: (b, 0)),
        scratch_shapes=[pltpu.VMEM((SB_HEADS * t, 1), F32), pltpu.VMEM((SB_HEADS * t, SB_HEAD_DIM), F32)],
    )
    return pl.pallas_call(
        functools.partial(_sb_sample_kernel, t=t, npp=npp),
        grid_spec=grid_spec,
        out_shape=jax.ShapeDtypeStruct((batch * t, SB_WIDTH), F32),
        compiler_params=_cparams(("parallel", "arbitrary")),
        name="sb_sample",
    )(pt, proj, proj, proj, *([ck] * npp), *([cv] * npp))


def _rope_tables(pos):
    half = MLA_ROPE // 2
    inv_freq = ROPE_THETA ** (-jnp.arange(half, dtype=F32) / half)
    ang = pos.astype(F32)[:, None] * inv_freq[None, :]
    cos, sin = jnp.cos(ang), jnp.sin(ang)
    zero = jnp.zeros_like(sin)
    reps = LANES // MLA_ROPE
    return (jnp.tile(jnp.concatenate([cos, cos], axis=1), (1, reps)),
            jnp.tile(jnp.concatenate([-sin, zero], axis=1), (1, reps)),
            jnp.tile(jnp.concatenate([zero, sin], axis=1), (1, reps)))


def _rope128(x, c, s_lo, s_hi):
    half = MLA_ROPE // 2
    return x * c + pltpu.roll(x, LANES - half, 1) * s_lo + pltpu.roll(x, half, 1) * s_hi


def _mla_prep_kernel(q_ref, ckv_ref, kr_ref, kvn_ref, wuk_ref, c_ref, slo_ref, shi_ref,
                     qlat_ref, qrope_ref, ckvn_ref, krr_ref):
    c, slo, shi = c_ref[...], slo_ref[...], shi_ref[...]
    for h in range(MLA_HEADS):
        qn = q_ref[:, h * MLA_NOPE:(h + 1) * MLA_NOPE].astype(BF16)
        qlat_ref[:, h * MLA_KV_LORA:(h + 1) * MLA_KV_LORA] = _dot(qn, wuk_ref[h])
    rope0 = MLA_HEADS * MLA_NOPE
    for i in range(MLA_HEADS * MLA_ROPE // LANES):
        qrope_ref[:, i * LANES:(i + 1) * LANES] = _rope128(
            q_ref[:, rope0 + i * LANES:rope0 + (i + 1) * LANES], c, slo, shi)
    ckv = ckv_ref[...]
    ms = jnp.mean(ckv * ckv, axis=-1, keepdims=True)
    ckvn_ref[...] = ckv * lax.rsqrt(ms + NORM_EPS) * kvn_ref[...]
    krr_ref[...] = _rope128(kr_ref[:, :LANES], c, slo, shi)[:, :MLA_ROPE]


def _mla_prep(q, proj, kv_norm, wuk_t, tabs):
    m = q.shape[0]
    tm = min(m, 512)
    qw = MLA_HEADS * (MLA_NOPE + MLA_ROPE)
    row = lambda w: pl.BlockSpec((tm, w), lambda i: (i, 0))
    return pl.pallas_call(
        _mla_prep_kernel,
        grid=(m // tm,),
        in_specs=[
            row(qw),
            pl.BlockSpec((tm, MLA_KV_LORA), lambda i: (i, MLA_Q_LORA // MLA_KV_LORA)),
            pl.BlockSpec((tm, MLA_KV_LORA), lambda i: (i, MLA_Q_LORA // MLA_KV_LORA + 1)),
            pl.BlockSpec((1, MLA_KV_LORA), lambda i: (0, 0)),
            pl.BlockSpec((MLA_HEADS, MLA_NOPE, MLA_KV_LORA), lambda i: (0, 0, 0)),
            row(LANES), row(LANES), row(LANES),
        ],
        out_specs=[row(MLA_HEADS * MLA_KV_LORA), row(MLA_HEADS * MLA_ROPE), row(MLA_KV_LORA), row(MLA_ROPE)],
        out_shape=[
            jax.ShapeDtypeStruct((m, MLA_HEADS * MLA_KV_LORA), F32),
            jax.ShapeDtypeStruct((m, MLA_HEADS * MLA_ROPE), F32),
            jax.ShapeDtypeStruct((m, MLA_KV_LORA), F32),
            jax.ShapeDtypeStruct((m, MLA_ROPE), F32),
        ],
        compiler_params=_cparams(("parallel",)),
        name="mla_prep",
    )(q, proj, proj, kv_norm.reshape(1, MLA_KV_LORA), wuk_t, *tabs)


def _mla_stack_q(qlat_ref, qrope_ref):
    ql = jnp.concatenate([qlat_ref[:, h * MLA_KV_LORA:(h + 1) * MLA_KV_LORA] for h in range(MLA_HEADS)],
                         axis=0).astype(BF16)
    qr = jnp.concatenate([qrope_ref[:, h * MLA_ROPE:(h + 1) * MLA_ROPE] for h in range(MLA_HEADS)],
                         axis=0).astype(BF16)
    return ql, qr


def _mla_update(ql, qr, ckv, kr, valid, m, l, acc):
    ckv = ckv.astype(BF16)
    s = (_dot_nt(ql, ckv) + _dot_nt(qr, kr.astype(BF16))) * MLA_SCALE
    if valid is not None:
        s = jnp.where(valid, s, NEG_BIG)
    m_new = jnp.maximum(m, jnp.max(s, axis=-1, keepdims=True))
    corr = jnp.exp(m - m_new)
    p = jnp.exp(s - m_new)
    return m_new, l * corr + jnp.sum(p, axis=-1, keepdims=True), acc * corr + _dot(p.astype(BF16), ckv)


def _mla_finish(acc, l, wuv_ref, o_ref, t):
    o_lat = (acc / l).astype(BF16)
    for h in range(MLA_HEADS):
        o_ref[:, h * MLA_V:(h + 1) * MLA_V] = _dot(o_lat[h * t:(h + 1) * t], wuv_ref[h])


def _mla_prompt_kernel(qi_ref, kj_ref, first_ref, last_ref, qlat_ref, qrope_ref, ckv_ref, kr_ref, wuv_ref,
                       o_ref, ql_ref, qr_ref, m_ref, l_ref, acc_ref, *, tq, tk):
    s = pl.program_id(1)
    rows = MLA_HEADS * tq

    @pl.when(first_ref[s] == 1)
    def _():
        ql, qr = _mla_stack_q(qlat_ref, qrope_ref)
        ql_ref[...] = ql
        qr_ref[...] = qr
        m_ref[...] = jnp.full_like(m_ref, NEG_BIG)
        l_ref[...] = jnp.zeros_like(l_ref)
        acc_ref[...] = jnp.zeros_like(acc_ref)

    qpos = qi_ref[s] * tq + _row_mod((rows, tk), tq)
    kpos = kj_ref[s] * tk + lax.broadcasted_iota(jnp.int32, (rows, tk), 1)
    m, l, acc = _mla_update(ql_ref[...], qr_ref[...], ckv_ref[...], kr_ref[...], kpos <= qpos,
                            m_ref[...], l_ref[...], acc_ref[...])
    m_ref[...] = m
    l_ref[...] = l
    acc_ref[...] = acc

    @pl.when(last_ref[s] == 1)
    def _():
        _mla_finish(acc, l, wuv_ref, o_ref, tq)


def _mla_prompt(qlat, qrope, ckvn, krr, wuv_t, batch, seq):
    tq = min(seq, 64)
    tk = min(seq, 512)
    tabs = _causal_steps(seq, tq, tk, descending=False)
    nsteps = tabs[0].shape[0]
    nq, nkb = seq // tq, seq // tk
    rows = MLA_HEADS * tq
    grid_spec = pltpu.PrefetchScalarGridSpec(
        num_scalar_prefetch=4,
        grid=(batch, nsteps),
        in_specs=[
            pl.BlockSpec((tq, MLA_HEADS * MLA_KV_LORA), lambda b, s, qi, kj, f, l: (b * nq + qi[s], 0)),
            pl.BlockSpec((tq, MLA_HEADS * MLA_ROPE), lambda b, s, qi, kj, f, l: (b * nq + qi[s], 0)),
            pl.BlockSpec((tk, MLA_KV_LORA), lambda b, s, qi, kj, f, l: (b * nkb + kj[s], 0)),
            pl.BlockSpec((tk, MLA_ROPE), lambda b, s, qi, kj, f, l: (b * nkb + kj[s], 0)),
            pl.BlockSpec((MLA_HEADS, MLA_KV_LORA, MLA_V), lambda b, s, qi, kj, f, l: (0, 0, 0)),
        ],
        out_specs=pl.BlockSpec((tq, MLA_WIDTH), lambda b, s, qi, kj, f, l: (b * nq + qi[s], 0)),
        scratch_shapes=[pltpu.VMEM((rows, MLA_KV_LORA), BF16), pltpu.VMEM((rows, MLA_ROPE), BF16),
                        pltpu.VMEM((rows, 1), F32), pltpu.VMEM((rows, 1), F32),
                        pltpu.VMEM((rows, MLA_KV_LORA), F32)],
    )
    return pl.pallas_call(
        functools.partial(_mla_prompt_kernel, tq=tq, tk=tk),
        grid_spec=grid_spec,
        out_shape=jax.ShapeDtypeStruct((batch * seq, MLA_WIDTH), F32),
        compiler_params=_cparams(("parallel", "arbitrary")),
        name="mla_prompt",
    )(*tabs, qlat, qrope, ckvn, krr, wuv_t)


def _mla_sample_kernel(pt_ref, qlat_ref, qrope_ref, cn_ref, kn_ref, wuv_ref, *refs, t, npp):
    c_refs, k_refs = refs[:npp], refs[npp:2 * npp]
    o_ref, m_ref, l_ref, acc_ref = refs[2 * npp:]
    c = pl.program_id(1)
    rows = MLA_HEADS * t
    ql, qr = _mla_stack_q(qlat_ref, qrope_ref)

    @pl.when(c == 0)
    def _():
        cn = jnp.concatenate([cn_ref[...], jnp.zeros((LANES - t, MLA_KV_LORA), F32)], axis=0)
        kn = jnp.concatenate([kn_ref[...], jnp.zeros((LANES - t, MLA_ROPE), F32)], axis=0)
        tq = _row_mod((rows, LANES), t)
        kidx = lax.broadcasted_iota(jnp.int32, (rows, LANES), 1)
        m, l, acc = _mla_update(ql, qr, cn, kn, kidx <= tq, jnp.full((rows, 1), NEG_BIG, F32),
                                jnp.zeros((rows, 1), F32), jnp.zeros((rows, MLA_KV_LORA), F32))
        m_ref[...] = m
        l_ref[...] = l
        acc_ref[...] = acc

    m, l, acc = m_ref[...], l_ref[...], acc_ref[...]
    for p in range(npp):
        m, l, acc = _mla_update(ql, qr, c_refs[p][0], k_refs[p][0], None, m, l, acc)
    m_ref[...] = m
    l_ref[...] = l
    acc_ref[...] = acc

    @pl.when(c == pl.num_programs(1) - 1)
    def _():
        _mla_finish(acc, l, wuv_ref, o_ref, t)


def _mla_sample(qlat, qrope, ckvn, krr, wuv_t, cache_ckv, cache_kr, j, page_table, batch, t):
    n_pool, page = cache_ckv.shape[1], cache_ckv.shape[2]
    assert page == LANES
    n_pages = page_table.shape[1]
    npp = _pages_per_step(n_pages)
    cc = cache_ckv.reshape(-1, page, MLA_KV_LORA)
    ck = cache_kr.reshape(-1, page, MLA_ROPE)
    pt = (page_table + j * n_pool).reshape(-1)
    rows = MLA_HEADS * t

    def page_spec(p, width):
        return pl.BlockSpec((1, page, width), lambda b, c, pt: (pt[b * n_pages + c * npp + p], 0, 0))

    grid_spec = pltpu.PrefetchScalarGridSpec(
        num_scalar_prefetch=1,
        grid=(batch, n_pages // npp),
        in_specs=[
            pl.BlockSpec((t, MLA_HEADS * MLA_KV_LORA), lambda b, c, pt: (b, 0)),
            pl.BlockSpec((t, MLA_HEADS * MLA_ROPE), lambda b, c, pt: (b, 0)),
            pl.BlockSpec((t, MLA_KV_LORA), lambda b, c, pt: (b, 0)),
            pl.BlockSpec((t, MLA_ROPE), lambda b, c, pt: (b, 0)),
            pl.BlockSpec((MLA_HEADS, MLA_KV_LORA, MLA_V), lambda b, c, pt: (0, 0, 0)),
        ] + [page_spec(p, MLA_KV_LORA) for p in range(npp)] + [page_spec(p, MLA_ROPE) for p in range(npp)],
        out_specs=pl.BlockSpec((t, MLA_WIDTH), lambda b, c, pt: (b, 0)),
        scratch_shapes=[pltpu.VMEM((rows, 1), F32), pltpu.VMEM((rows, 1), F32),
                        pltpu.VMEM((rows, MLA_KV_LORA), F32)],
    )
    return pl.pallas_call(
        functools.partial(_mla_sample_kernel, t=t, npp=npp),
        grid_spec=grid_spec,
        out_shape=jax.ShapeDtypeStruct((batch * t, MLA_WIDTH), F32),
        compiler_params=_cparams(("parallel", "arbitrary")),
        name="mla_sample",
    )(pt, qlat, qrope, ckvn, krr, wuv_t, *([cc] * npp), *([ck] * npp))


def kernel(x_prompt, x_sample, state_ssd_conv, state_ssd_h, cache_sb_k, cache_sb_v, cache_mla_ckv,
           cache_mla_krope, page_table, norm_pre, norm_post, ssd_w_in, ssd_conv_w, ssd_conv_b, ssd_dt_bias,
           ssd_a_log, ssd_d, ssd_norm_w, ssd_w_out, sb_w_in, sb_w_out, mla_w_in, mla_q_norm, mla_kv_norm,
           mla_w_uq, mla_w_uk, mla_w_uv, mla_w_out):
    bp, seq, d = x_prompt.shape
    bs, t, _ = x_sample.shape
    depth = norm_pre.shape[0]
    past = page_table.shape[1] * cache_sb_k.shape[2]
    assert seq % SSD_CHUNK == 0 and t >= SSD_CONV - 1
    xp = x_prompt.reshape(bp * seq, d)
    xs = x_sample.reshape(bs * t, d)
    outs = {k: [] for k in ("p_conv", "p_h", "p_k", "p_v", "p_ckv", "p_kr",
                            "s_conv", "s_h", "s_k", "s_v", "s_ckv", "s_kr")}
    zx = SSD_D_INNER + SSD_CONV_DIM
    for i in range(depth):
        kind, j = i % N_MIXERS, i // N_MIXERS
        if kind == 0:
            w_zx = ssd_w_in[j][:, :zx].astype(BF16)
            w_dt = jnp.pad(ssd_w_in[j][:, zx:], ((0, 0), (0, LANES - SSD_HEADS))).astype(BF16)
            w_out = ssd_w_out[j].astype(BF16)
            consts = _ssd_consts(ssd_conv_w[j], ssd_conv_b[j], ssd_dt_bias[j], ssd_a_log[j], ssd_d[j],
                                 ssd_norm_w[j])
            proj_p = _norm_matmul(xp, norm_pre[i], w_zx, name="ssd_in_p")
            proj_s = _norm_matmul(xs, norm_pre[i], w_zx, name="ssd_in_s")
            dt_p = _norm_matmul(xp, norm_pre[i], w_dt, name="ssd_dt_p")
            dt_s = _norm_matmul(xs, norm_pre[i], w_dt, name="ssd_dt_s")
            y_p, h_p = _ssd_prompt(proj_p, dt_p, consts, bp, seq)
            h0 = state_ssd_h[j].reshape(bs, SSD_D_INNER, SSD_STATE)
            y_s, h_s = _ssd_sample(proj_s, dt_s, state_ssd_conv[j], h0, consts, bs, t)
            outs["p_conv"].append(proj_p.reshape(bp, seq, zx)[:, seq - (SSD_CONV - 1):, SSD_D_INNER:])
            outs["s_conv"].append(proj_s.reshape(bs, t, zx)[:, t - (SSD_CONV - 1):, SSD_D_INNER:])
            outs["p_h"].append(h_p.reshape(bp, SSD_HEADS, SSD_HEAD_DIM, SSD_STATE))
            outs["s_h"].append(h_s.reshape(bs, SSD_HEADS, SSD_HEAD_DIM, SSD_STATE))
            xp = _matmul_post(y_p, w_out, xp, norm_post[i], name="ssd_out_p")
            xs = _matmul_post(y_s, w_out, xs, norm_post[i], name="ssd_out_s")
        elif kind == 1:
            w_in = sb_w_in[j].astype(BF16)
            w_out = sb_w_out[j].astype(BF16)
            proj_p = _norm_matmul(xp, norm_pre[i], w_in, name="sb_in_p")
            proj_s = _norm_matmul(xs, norm_pre[i], w_in, name="sb_in_s")
            gcol = SB_WIDTH + 2 * SB_KV_WIDTH
            a_p = _sb_prompt(proj_p, bp, seq)
            a_s = _sb_sample(proj_s, cache_sb_k, cache_sb_v, j, page_table, bs, t)
            for tag, pr, nb, ln in (("p", proj_p, bp, seq), ("s", proj_s, bs, t)):
                kv = pr.reshape(nb, ln, -1)
                outs[tag + "_k"].append(kv[:, :, SB_WIDTH:SB_WIDTH + SB_KV_WIDTH].reshape(nb, ln, SB_KV_HEADS, SB_HEAD_DIM))
                outs[tag + "_v"].append(kv[:, :, SB_WIDTH + SB_KV_WIDTH:gcol].reshape(nb, ln, SB_KV_HEADS, SB_HEAD_DIM))
            xp = _matmul_post(a_p, w_out, xp, norm_post[i], gate=proj_p, gate_col=gcol, name="sb_out_p")
            xs = _matmul_post(a_s, w_out, xs, norm_post[i], gate=proj_s, gate_col=gcol, name="sb_out_s")
        else:
            o1, o2, o3 = MLA_Q_LORA, MLA_Q_LORA + MLA_KV_LORA, MLA_Q_LORA + MLA_KV_LORA + MLA_ROPE
            w = mla_w_in[j]
            w_in = jnp.concatenate([w[:, :o3], jnp.zeros((d, MLA_KV_LORA - MLA_ROPE), F32), w[:, o3:]],
                                   axis=1).astype(BF16)
            gcol = o2 + MLA_KV_LORA
            uq = mla_w_uq[j]
            w_uq = jnp.concatenate([uq[:, :, :MLA_NOPE].reshape(MLA_Q_LORA, -1),
                                    uq[:, :, MLA_NOPE:].reshape(MLA_Q_LORA, -1)], axis=1).astype(BF16)
            wuk_t = jnp.transpose(mla_w_uk[j], (1, 2, 0)).astype(BF16)
            wuv_t = jnp.transpose(mla_w_uv[j], (1, 0, 2)).astype(BF16)
            w_out = mla_w_out[j].astype(BF16)
            res = []
            for tag, xx, nb, ln, pos in (("p", xp, bp, seq, jnp.arange(seq)), ("s", xs, bs, t, past + jnp.arange(t))):
                proj = _norm_matmul(xx, norm_pre[i], w_in, name="mla_in_" + tag)
                q = _norm_matmul(proj, mla_q_norm[j], w_uq, name="mla_uq_" + tag)
                tabs = tuple(jnp.tile(tb, (nb, 1)) for tb in _rope_tables(pos))
                qlat, qrope, ckvn, krr = _mla_prep(q, proj, mla_kv_norm[j], wuk_t, tabs)
                outs[tag + "_ckv"].append(ckvn.reshape(nb, ln, MLA_KV_LORA))
                outs[tag + "_kr"].append(krr.reshape(nb, ln, MLA_ROPE))
                if tag == "p":
                    o = _mla_prompt(qlat, qrope, ckvn, krr, wuv_t, nb, ln)
                else:
                    o = _mla_sample(qlat, qrope, ckvn, krr, wuv_t, cache_mla_ckv, cache_mla_krope, j,
                                    page_table, nb, ln)
                res.append(_matmul_post(o, w_out, xx, norm_post[i], gate=proj, gate_col=gcol,
                                        name="mla_out_" + tag))
            xp, xs = res
    st = lambda k: jnp.stack(outs[k])
    return (xp.reshape(bp, seq, d), xs.reshape(bs, t, d),
            st("p_conv"), st("p_h"), st("p_k"), st("p_v"), st("p_ckv"), st("p_kr"),
            st("s_conv"), st("s_h"), st("s_k"), st("s_v"), st("s_ckv"), st("s_kr"))
```

```python
import functools
import math

import numpy as np
import jax
import jax.numpy as jnp
from jax import lax
from jax.experimental import pallas as pl
from jax.experimental.pallas import tpu as pltpu

F32 = jnp.float32
BF16 = jnp.bfloat16

D_MODEL = 2048
N_MIXERS = 3
NORM_EPS = 1e-6

SSD_D_INNER = 2 * D_MODEL
SSD_HEAD_DIM = 64
SSD_HEADS = SSD_D_INNER // SSD_HEAD_DIM
SSD_GROUPS = 8
SSD_STATE = 128
SSD_CONV = 4
SSD_CHUNK = 128
SSD_BC = SSD_GROUPS * SSD_STATE
SSD_CONV_DIM = SSD_D_INNER + 2 * SSD_BC
SSD_GROUP_WIDTH = SSD_D_INNER // SSD_GROUPS
SSD_HEADS_PER_GROUP = SSD_HEADS // SSD_GROUPS

SB_HEADS = 16
SB_KV_HEADS = 4
SB_GROUP = SB_HEADS // SB_KV_HEADS
SB_HEAD_DIM = 128
SB_WIDTH = SB_HEADS * SB_HEAD_DIM
SB_KV_WIDTH = SB_KV_HEADS * SB_HEAD_DIM
SB_SCALE = SB_HEAD_DIM ** -0.5

MLA_HEADS = 16
MLA_Q_LORA = 512
MLA_KV_LORA = 256
MLA_NOPE = 128
MLA_ROPE = 64
MLA_V = 128
MLA_WIDTH = MLA_HEADS * MLA_V
MLA_SCALE = (MLA_NOPE + MLA_ROPE) ** -0.5
ROPE_THETA = 10000.0

LANES = 128
SUBLANES = 8
VMEM_LIMIT_BYTES = 48 * 1024 * 1024
NEG_BIG = -1e30


def _cparams(sem):
    return pltpu.CompilerParams(dimension_semantics=sem, vmem_limit_bytes=VMEM_LIMIT_BYTES)


def _dot(a, b):
    return jnp.dot(a, b, preferred_element_type=F32)


def _dot_nt(a, b):
    return lax.dot_general(a, b, (((1,), (1,)), ((), ())), preferred_element_type=F32)


def _dot_tn(a, b):
    return lax.dot_general(a, b, (((0,), (0,)), ((), ())), preferred_element_type=F32)


def _split_bf16(a, terms):
    parts, r = [], a
    for i in range(terms):
        p = r.astype(BF16)
        parts.append(p)
        if i + 1 < terms:
            r = r - p.astype(F32)
    return parts


def _dot_sel_rhs(a, sel, terms):
    out = None
    for p in _split_bf16(a, terms):
        d = _dot(p, sel)
        out = d if out is None else out + d
    return out


def _dot_sel_lhs(sel, a, terms):
    out = None
    for p in _split_bf16(a, terms):
        d = _dot(sel, p)
        out = d if out is None else out + d
    return out


def _row_mod(shape, n):
    assert n & (n - 1) == 0
    return lax.broadcasted_iota(jnp.int32, shape, 0) & (n - 1)


def _softplus(x):
    return jnp.maximum(x, 0.0) + jnp.log1p(jnp.exp(-jnp.abs(x)))


def _silu(x):
    return x * jax.nn.sigmoid(x)


def _norm_matmul_kernel(x_ref, nw_ref, w_ref, o_ref, xn_ref):
    @pl.when(pl.program_id(1) == 0)
    def _():
        x = x_ref[...]
        ms = jnp.mean(x * x, axis=-1, keepdims=True)
        xn_ref[...] = (x * lax.rsqrt(ms + NORM_EPS) * nw_ref[...]).astype(BF16)

    o_ref[...] = _dot(xn_ref[...], w_ref[...])


def _norm_matmul(x, nw, w, *, col_block=0, name):
    m = x.shape[0]
    k, n = w.shape
    tm = min(m, 1024)
    tn = min(n, 512)
    return pl.pallas_call(
        _norm_matmul_kernel,
        grid=(m // tm, n // tn),
        in_specs=[
            pl.BlockSpec((tm, k), lambda i, j: (i, col_block)),
            pl.BlockSpec((1, k), lambda i, j: (0, 0)),
            pl.BlockSpec((k, tn), lambda i, j: (0, j)),
        ],
        out_specs=pl.BlockSpec((tm, tn), lambda i, j: (i, j)),
        out_shape=jax.ShapeDtypeStruct((m, n), F32),
        scratch_shapes=[pltpu.VMEM((tm, k), BF16)],
        compiler_params=_cparams(("parallel", "arbitrary")),
        name=name,
    )(x, nw.reshape(1, k), w)


def _post_kernel(*refs, gated, nk):
    if gated:
        a_ref, g_ref, w_ref, res_ref, pw_ref, o_ref, acc_ref = refs
    else:
        a_ref, w_ref, res_ref, pw_ref, o_ref, acc_ref = refs
    kk = pl.program_id(1)
    a = a_ref[...].astype(F32)
    if gated:
        a = a * _silu(g_ref[...])
    part = _dot(a.astype(BF16), w_ref[...])

    @pl.when(kk == 0)
    def _():
        acc_ref[...] = part

    @pl.when(kk > 0)
    def _():
        acc_ref[...] += part

    @pl.when(kk == nk - 1)
    def _():
        y = acc_ref[...]
        ms = jnp.mean(y * y, axis=-1, keepdims=True)
        o_ref[...] = res_ref[...] + y * lax.rsqrt(ms + NORM_EPS) * pw_ref[...]


def _matmul_post(a, w, res, pw, *, gate=None, gate_col=0, name):
    m, k = a.shape
    n = w.shape[1]
    tm = min(m, 512)
    tk = min(k, 512)
    nk = k // tk
    gated = gate is not None
    in_specs = [pl.BlockSpec((tm, tk), lambda i, kk: (i, kk))]
    args = [a]
    if gated:
        gb = gate_col // tk
        in_specs.append(pl.BlockSpec((tm, tk), lambda i, kk: (i, gb + kk)))
        args.append(gate)
    in_specs += [
        pl.BlockSpec((tk, n), lambda i, kk: (kk, 0)),
        pl.BlockSpec((tm, n), lambda i, kk: (i, 0)),
        pl.BlockSpec((1, n), lambda i, kk: (0, 0)),
    ]
    args += [w, res, pw.reshape(1, n)]
    return pl.pallas_call(
        functools.partial(_post_kernel, gated=gated, nk=nk),
        grid=(m // tm, nk),
        in_specs=in_specs,
        out_specs=pl.BlockSpec((tm, n), lambda i, kk: (i, 0)),
        out_shape=jax.ShapeDtypeStruct((m, n), F32),
        scratch_shapes=[pltpu.VMEM((tm, n), F32)],
        compiler_params=_cparams(("parallel", "arbitrary")),
        name=name,
    )(*args)


def _conv_taps(ext, cur, w, bias, rows):
    off = SUBLANES - (SSD_CONV - 1)
    acc = ext[off:off + rows] * w[0:1]
    for k in range(1, SSD_CONV - 1):
        acc = acc + ext[off + k:off + k + rows] * w[k:k + 1]
    acc = acc + cur * w[SSD_CONV - 1:SSD_CONV]
    return _silu(acc + bias)


def _gate_norm(y, z, nw):
    y = y * _silu(z)
    ms = jnp.mean(y * y, axis=-1, keepdims=True)
    return y * lax.rsqrt(ms + NORM_EPS) * nw


def _ssd_prompt_kernel(z_ref, xs_ref, bm_ref, cm_ref, dt_ref, cw_ref, cb_ref, dtb_ref, alog_ref,
                       dsk_ref, nw_ref, y_ref, h_ref, tail_ref):
    cs = SSD_CHUNK

    @pl.when(pl.program_id(1) == 0)
    def _():
        h_ref[...] = jnp.zeros_like(h_ref)
        tail_ref[...] = jnp.zeros_like(tail_ref)

    def conv(cur, lo, width):
        ext = jnp.concatenate([tail_ref[:, lo:lo + width], cur], axis=0)
        out = _conv_taps(ext, cur, cw_ref[:, lo:lo + width], cb_ref[:, lo:lo + width], cs)
        tail_ref[:, lo:lo + width] = cur[cs - SUBLANES:cs]
        return out

    row = lax.broadcasted_iota(jnp.int32, (cs, cs), 0)
    col = lax.broadcasted_iota(jnp.int32, (cs, cs), 1)
    causal = row >= col
    tril = causal.astype(BF16)
    left = col < SSD_HEAD_DIM

    dt = _softplus(dt_ref[...] + dtb_ref[...])
    la = dt * (-jnp.exp(alog_ref[...]))
    lc = _dot_sel_lhs(tril, la, 3)
    lc_t = lc.T
    dt_t = dt.T
    e_all = jnp.exp(lc)
    lc_last = lc[cs - 1:cs, :]
    de_all = jnp.exp(lc_last - lc) * dt
    cd_all = jnp.exp(lc_last)

    b_act = conv(bm_ref[...], SSD_D_INNER, SSD_BC)
    c_act = conv(cm_ref[...], SSD_D_INNER + SSD_BC, SSD_BC)

    for g in range(SSD_GROUPS):
        gl = g * SSD_GROUP_WIDTH
        xg = conv(xs_ref[:, gl:gl + SSD_GROUP_WIDTH], gl, SSD_GROUP_WIDTH)
        bg = b_act[:, g * SSD_STATE:(g + 1) * SSD_STATE].astype(BF16)
        cg = c_act[:, g * SSD_STATE:(g + 1) * SSD_STATE].astype(BF16)
        cbm = _dot_nt(cg, bg)
        h_prev = h_ref[0, gl:gl + SSD_GROUP_WIDTH, :]
        y_off = _dot_nt(cg, h_prev.astype(BF16))
        y_parts, xw_parts, cd_parts = [], [], []
        for pr in range(SSD_HEADS_PER_GROUP // 2):
            xp = xg[:, pr * LANES:(pr + 1) * LANES]
            ws, es, des = [], [], []
            for hh in (0, 1):
                h = g * SSD_HEADS_PER_GROUP + 2 * pr + hh
                seg = lc[:, h:h + 1] - lc_t[h:h + 1, :]
                decay = jnp.where(causal, jnp.exp(jnp.where(causal, seg, 0.0)), 0.0)
                ws.append(cbm * decay * dt_t[h:h + 1, :])
                es.append(jnp.broadcast_to(e_all[:, h:h + 1], (cs, LANES)))
                des.append(jnp.broadcast_to(de_all[:, h:h + 1], (cs, LANES)))
                cd_parts.append(jnp.broadcast_to(cd_all[:, h:h + 1], (SSD_HEAD_DIM, SSD_STATE)))
            w2 = jnp.concatenate(ws, axis=1).astype(BF16)
            x2 = jnp.concatenate([jnp.where(left, xp, 0.0), jnp.where(left, 0.0, xp)],
                                 axis=0).astype(BF16)
            y_diag = _dot(w2, x2)
            e2 = jnp.where(left, es[0], es[1])
            y_parts.append(y_diag + y_off[:, pr * LANES:(pr + 1) * LANES] * e2
                           + dsk_ref[:, gl + pr * LANES:gl + (pr + 1) * LANES] * xp)
            xw_parts.append(xp * jnp.where(left, des[0], des[1]))
        yg = jnp.concatenate(y_parts, axis=1)
        y_ref[:, gl:gl + SSD_GROUP_WIDTH] = _gate_norm(
            yg, z_ref[:, gl:gl + SSD_GROUP_WIDTH], nw_ref[:, gl:gl + SSD_GROUP_WIDTH]).astype(y_ref.dtype)
        xw = jnp.concatenate(xw_parts, axis=1).astype(BF16)
        states = _dot_tn(xw, bg)
        h_ref[0, gl:gl + SSD_GROUP_WIDTH, :] = h_prev * jnp.concatenate(cd_parts, axis=0) + states


def _ssd_consts(conv_w, conv_b, dt_bias, a_log, d_skip, norm_w):
    pad = LANES - SSD_HEADS
    return (conv_w, conv_b.reshape(1, SSD_CONV_DIM),
            jnp.pad(dt_bias, (0, pad)).reshape(1, LANES), jnp.pad(a_log, (0, pad)).reshape(1, LANES),
            jnp.repeat(d_skip, SSD_HEAD_DIM).reshape(1, SSD_D_INNER), norm_w.reshape(1, SSD_D_INNER))


def _ssd_prompt(proj, dtraw, consts, batch, seq):
    cs = SSD_CHUNK
    nc = seq // cs
    zb = SSD_D_INNER // SSD_BC
    const_specs = [
        pl.BlockSpec((SSD_CONV, SSD_CONV_DIM), lambda b, c: (0, 0)),
        pl.BlockSpec((1, SSD_CONV_DIM), lambda b, c: (0, 0)),
        pl.BlockSpec((1, LANES), lambda b, c: (0, 0)),
        pl.BlockSpec((1, LANES), lambda b, c: (0, 0)),
        pl.BlockSpec((1, SSD_D_INNER), lambda b, c: (0, 0)),
        pl.BlockSpec((1, SSD_D_INNER), lambda b, c: (0, 0)),
    ]
    return pl.pallas_call(
        _ssd_prompt_kernel,
        grid=(batch, nc),
        in_specs=[
            pl.BlockSpec((cs, SSD_D_INNER), lambda b, c: (b * nc + c, 0)),
            pl.BlockSpec((cs, SSD_D_INNER), lambda b, c: (b * nc + c, 1)),
            pl.BlockSpec((cs, SSD_BC), lambda b, c: (b * nc + c, 2 * zb)),
            pl.BlockSpec((cs, SSD_BC), lambda b, c: (b * nc + c, 2 * zb + 1)),
            pl.BlockSpec((cs, LANES), lambda b, c: (b * nc + c, 0)),
        ] + const_specs,
        out_specs=[
            pl.BlockSpec((cs, SSD_D_INNER), lambda b, c: (b * nc + c, 0)),
            pl.BlockSpec((1, SSD_D_INNER, SSD_STATE), lambda b, c: (b, 0, 0)),
        ],
        out_shape=[
            jax.ShapeDtypeStruct((batch * seq, SSD_D_INNER), BF16),
            jax.ShapeDtypeStruct((batch, SSD_D_INNER, SSD_STATE), F32),
        ],
        scratch_shapes=[pltpu.VMEM((SUBLANES, SSD_CONV_DIM), F32)],
        compiler_params=_cparams(("parallel", "arbitrary")),
        name="ssd_prompt",
    )(proj, proj, proj, proj, dtraw, *consts)


def _ssd_sample_kernel(z_ref, xs_ref, bm_ref, cm_ref, dt_ref, st_ref, h0_ref, cw_ref, cb_ref, dtb_ref,
                       alog_ref, dsk_ref, nw_ref, rexp_ref, gsel_ref, y_ref, h_ref):
    t = xs_ref.shape[0]
    pairs = t * t

    def conv(cur, lo, width):
        ext = jnp.concatenate([st_ref[0, :, lo:lo + width], cur], axis=0)
        return _conv_taps(ext, cur, cw_ref[:, lo:lo + width], cb_ref[:, lo:lo + width], t)

    def tile_t(a):
        return jnp.concatenate([a] * t, axis=0)

    def tile_s(a):
        return jnp.concatenate([jnp.broadcast_to(a[s:s + 1], a.shape) for s in range(t)], axis=0)

    x_act = conv(xs_ref[...], 0, SSD_D_INNER)
    b_act = conv(bm_ref[...], SSD_D_INNER, SSD_BC)
    c_act = conv(cm_ref[...], SSD_D_INNER + SSD_BC, SSD_BC)

    dt = _softplus(dt_ref[...] + dtb_ref[...])
    la = dt * (-jnp.exp(alog_ref[...]))
    rix = lax.broadcasted_iota(jnp.int32, (t, LANES), 0)
    lc = la
    sh = 1
    while sh < t:
        lc = lc + jnp.where(rix >= sh, pltpu.roll(lc, sh, 0), 0.0)
        sh *= 2
    lc_last = lc[t - 1:t, :]
    rexp = rexp_ref[...]

    r = lax.broadcasted_iota(jnp.int32, (pairs, LANES), 0)
    ok = lax.shift_right_logical(r, int(math.log2(t))) <= (r & (t - 1))
    decay = jnp.where(ok, jnp.exp(jnp.where(ok, tile_t(lc) - tile_s(lc), 0.0)), 0.0)
    cbe = _dot_sel_rhs(tile_t(c_act) * tile_s(b_act), gsel_ref[...], 3)
    coef = _dot_sel_rhs(cbe * decay * tile_s(dt), rexp, 3)
    y = dsk_ref[...] * x_act
    for s in range(t):
        y = y + coef[s * t:(s + 1) * t] * jnp.broadcast_to(x_act[s:s + 1], x_act.shape)

    e_exp = _dot_sel_rhs(jnp.exp(lc), rexp, 3)
    xw = x_act * _dot_sel_rhs(jnp.exp(lc_last - lc) * dt, rexp, 3)
    cd_all = jnp.exp(lc_last)
    zero_rows = jnp.zeros((LANES - t, SSD_GROUP_WIDTH), F32)
    zero_b = jnp.zeros((LANES - t, SSD_STATE), F32)
    for g in range(SSD_GROUPS):
        gl = g * SSD_GROUP_WIDTH
        bg = b_act[:, g * SSD_STATE:(g + 1) * SSD_STATE]
        cg = c_act[:, g * SSD_STATE:(g + 1) * SSD_STATE]
        h_prev = h0_ref[0, gl:gl + SSD_GROUP_WIDTH, :]
        cg16 = jnp.concatenate([cg, jnp.zeros_like(cg)], axis=0).astype(BF16)
        y_off = _dot_nt(cg16, h_prev.astype(BF16))[:t]
        yg = y[:, gl:gl + SSD_GROUP_WIDTH] + y_off * e_exp[:, gl:gl + SSD_GROUP_WIDTH]
        y_ref[:, gl:gl + SSD_GROUP_WIDTH] = _gate_norm(
            yg, z_ref[:, gl:gl + SSD_GROUP_WIDTH], nw_ref[:, gl:gl + SSD_GROUP_WIDTH]).astype(y_ref.dtype)
        xw_pad = jnp.concatenate([xw[:, gl:gl + SSD_GROUP_WIDTH], zero_rows], axis=0).astype(BF16)
        b_pad = jnp.concatenate([bg, zero_b], axis=0).astype(BF16)
        states = _dot_tn(xw_pad, b_pad)
        cd = jnp.concatenate(
            [jnp.broadcast_to(cd_all[:, h:h + 1], (SSD_HEAD_DIM, SSD_STATE))
             for h in range(g * SSD_HEADS_PER_GROUP, (g + 1) * SSD_HEADS_PER_GROUP)], axis=0)
        h_ref[0, gl:gl + SSD_GROUP_WIDTH, :] = h_prev * cd + states


def _ssd_sample(proj, dtraw, conv_state, h0, j, consts, batch, t):
    assert t == SUBLANES, "sample SSD path handles one sublane tile of new tokens"
    zb = SSD_D_INNER // SSD_BC
    hist = jnp.pad(conv_state, ((0, 0), (SUBLANES - (SSD_CONV - 1), 0), (0, 0)))
    heads = np.arange(LANES)
    rexp = (heads[:, None] == (np.arange(SSD_D_INNER) // SSD_HEAD_DIM)[None, :])
    gsel = ((np.arange(SSD_BC) // SSD_STATE)[:, None] == (heads // SSD_HEADS_PER_GROUP)[None, :])
    gsel = gsel & (heads < SSD_HEADS)[None, :]
    rexp = jnp.asarray(rexp, BF16)
    gsel = jnp.asarray(gsel, BF16)
    full = lambda shape: pl.BlockSpec(shape, lambda b: (0,) * len(shape))
    return pl.pallas_call(
        _ssd_sample_kernel,
        grid=(batch,),
        in_specs=[
            pl.BlockSpec((t, SSD_D_INNER), lambda b: (b, 0)),
            pl.BlockSpec((t, SSD_D_INNER), lambda b: (b, 1)),
            pl.BlockSpec((t, SSD_BC), lambda b: (b, 2 * zb)),
            pl.BlockSpec((t, SSD_BC), lambda b: (b, 2 * zb + 1)),
            pl.BlockSpec((t, LANES), lambda b: (b, 0)),
            pl.BlockSpec((1, SUBLANES, SSD_CONV_DIM), lambda b: (b, 0, 0)),
            pl.BlockSpec((1, SSD_D_INNER, SSD_STATE), lambda b: (j * batch + b, 0, 0)),
            full((SSD_CONV, SSD_CONV_DIM)), full((1, SSD_CONV_DIM)), full((1, LANES)), full((1, LANES)),
            full((1, SSD_D_INNER)), full((1, SSD_D_INNER)),
            full((LANES, SSD_D_INNER)), full((SSD_BC, LANES)),
        ],
        out_specs=[
            pl.BlockSpec((t, SSD_D_INNER), lambda b: (b, 0)),
            pl.BlockSpec((1, SSD_D_INNER, SSD_STATE), lambda b: (b, 0, 0)),
        ],
        out_shape=[
            jax.ShapeDtypeStruct((batch * t, SSD_D_INNER), F32),
            jax.ShapeDtypeStruct((batch, SSD_D_INNER, SSD_STATE), F32),
        ],
        compiler_params=_cparams(("parallel",)),
        name="ssd_sample",
    )(proj, proj, proj, proj, dtraw, hist, h0, *consts, rexp, gsel)


def _causal_steps(seq, tq, tk, descending):
    qi, kj, first, last = [], [], [], []
    for i in range(seq // tq):
        n = (i * tq + tq - 1) // tk + 1
        order = range(n - 1, -1, -1) if descending else range(n)
        for a, j in enumerate(order):
            qi.append(i), kj.append(j), first.append(int(a == 0)), last.append(int(a == n - 1))
    return tuple(jnp.asarray(np.asarray(v, np.int32)) for v in (qi, kj, first, last))


def _sb_weights(z, valid, ls, usel, latest_first):
    rows, nk = z.shape
    nb = nk // LANES
    sp = _softplus(z)
    l = -sp if valid is None else jnp.where(valid, -sp, 0.0)
    blocks = [l[:, b * LANES:(b + 1) * LANES] for b in range(nb)]
    later_in = _dot_sel_rhs(jnp.concatenate(blocks, axis=0), usel, 2)
    carry = [None] * nb
    for b in (range(nb) if latest_first else range(nb - 1, -1, -1)):
        carry[b] = ls
        ls = ls + jnp.sum(blocks[b], axis=-1, keepdims=True)
    later = jnp.concatenate([later_in[b * rows:(b + 1) * rows] + carry[b] for b in range(nb)], axis=1)
    w = jnp.exp(z - sp + later)
    if valid is not None:
        w = jnp.where(valid, w, 0.0)
    return w.astype(BF16), ls


def _suffix_sel():
    a = lax.broadcasted_iota(jnp.int32, (LANES, LANES), 0)
    b = lax.broadcasted_iota(jnp.int32, (LANES, LANES), 1)
    return (a > b).astype(BF16)


def _sb_prompt_kernel(qi_ref, kj_ref, first_ref, last_ref, q_ref, k_ref, v_ref, o_ref,
                      ls_ref, acc_ref, *, tq):
    s = pl.program_id(2)

    @pl.when(first_ref[s] == 1)
    def _():
        ls_ref[...] = jnp.zeros_like(ls_ref)
        acc_ref[...] = jnp.zeros_like(acc_ref)

    kb = k_ref[...].astype(BF16)
    vb = v_ref[...].astype(BF16)
    usel = _suffix_sel()

    def step(valid):
        for g in range(SB_GROUP):
            r = slice(g * tq, (g + 1) * tq)
            qg = q_ref[:, g * SB_HEAD_DIM:(g + 1) * SB_HEAD_DIM].astype(BF16)
            w, ls = _sb_weights(_dot_nt(qg, kb) * SB_SCALE, valid, ls_ref[r], usel, latest_first=False)
            ls_ref[r] = ls
            acc_ref[r] += _dot(w, vb)

    on_diagonal = kj_ref[s] == qi_ref[s]

    @pl.when(on_diagonal)
    def _():
        step(lax.broadcasted_iota(jnp.int32, (tq, tq), 1) < lax.broadcasted_iota(jnp.int32, (tq, tq), 0))

    @pl.when(jnp.logical_not(on_diagonal))
    def _():
        step(None)

    @pl.when(last_ref[s] == 1)
    def _():
        for g in range(SB_GROUP):
            o_ref[:, g * SB_HEAD_DIM:(g + 1) * SB_HEAD_DIM] = acc_ref[g * tq:(g + 1) * tq]


def _sb_prompt(proj, batch, seq):
    tq = min(seq, 256)
    tk = tq
    tabs = _causal_steps(seq, tq, tk, descending=True)
    nsteps = tabs[0].shape[0]
    nq, nkb = seq // tq, seq // tk
    kcol = SB_WIDTH // SB_HEAD_DIM
    vcol = (SB_WIDTH + SB_KV_WIDTH) // SB_HEAD_DIM
    rows = SB_GROUP * tq
    grid_spec = pltpu.PrefetchScalarGridSpec(
        num_scalar_prefetch=4,
        grid=(batch, SB_KV_HEADS, nsteps),
        in_specs=[
            pl.BlockSpec((tq, SB_GROUP * SB_HEAD_DIM), lambda b, h, s, qi, kj, f, l: (b * nq + qi[s], h)),
            pl.BlockSpec((tk, SB_HEAD_DIM), lambda b, h, s, qi, kj, f, l: (b * nkb + kj[s], kcol + h)),
            pl.BlockSpec((tk, SB_HEAD_DIM), lambda b, h, s, qi, kj, f, l: (b * nkb + kj[s], vcol + h)),
        ],
        out_specs=pl.BlockSpec((tq, SB_GROUP * SB_HEAD_DIM), lambda b, h, s, qi, kj, f, l: (b * nq + qi[s], h)),
        scratch_shapes=[pltpu.VMEM((rows, 1), F32), pltpu.VMEM((rows, SB_HEAD_DIM), F32)],
    )
    return pl.pallas_call(
        functools.partial(_sb_prompt_kernel, tq=tq),
        grid_spec=grid_spec,
        out_shape=jax.ShapeDtypeStruct((batch * seq, SB_WIDTH), F32),
        compiler_params=_cparams(("parallel", "parallel", "arbitrary")),
        name="sb_prompt",
    )(*tabs, proj, proj, proj)


def _sb_sample_kernel(pt_ref, q_ref, kn_ref, vn_ref, *refs, t, npp):
    k_refs, v_refs = refs[:npp], refs[npp:2 * npp]
    o_ref, ls_ref, acc_ref = refs[2 * npp:]
    c = pl.program_id(1)
    rows = SB_HEADS * t
    usel = _suffix_sel()
    qs = jnp.concatenate([q_ref[:, h * SB_HEAD_DIM:(h + 1) * SB_HEAD_DIM] for h in range(SB_HEADS)],
                         axis=0).astype(BF16)
    grows = SB_GROUP * t

    def attend(keys, values, valid, ls, acc):
        z = jnp.concatenate([_dot_nt(qs[h * grows:(h + 1) * grows], keys[h])
                             for h in range(SB_KV_HEADS)], axis=0) * SB_SCALE
        w, ls = _sb_weights(z, valid, ls, usel, latest_first=True)
        pv = jnp.concatenate([_dot(w[h * grows:(h + 1) * grows], values[h])
                              for h in range(SB_KV_HEADS)], axis=0)
        return ls, acc + pv

    @pl.when(c == 0)
    def _():
        pad = jnp.zeros((LANES - t, SB_HEAD_DIM), F32)

        def new_tokens(ref):
            return [jnp.concatenate([ref[:, h * SB_HEAD_DIM:(h + 1) * SB_HEAD_DIM], pad], axis=0).astype(BF16)
                    for h in range(SB_KV_HEADS)]

        tq = _row_mod((rows, LANES), t)
        kidx = lax.broadcasted_iota(jnp.int32, (rows, LANES), 1)
        ls, acc = attend(new_tokens(kn_ref), new_tokens(vn_ref), kidx < tq,
                         jnp.zeros((rows, 1), F32), jnp.zeros((rows, SB_HEAD_DIM), F32))
        ls_ref[...] = ls
        acc_ref[...] = acc

    def pages(page_refs):
        return [jnp.concatenate([r[0, pl.ds(h, LANES, stride=SB_KV_HEADS), :] for r in page_refs],
                                axis=0).astype(BF16) for h in range(SB_KV_HEADS)]

    ls, acc = attend(pages(k_refs), pages(v_refs), None, ls_ref[...], acc_ref[...])
    ls_ref[...] = ls
    acc_ref[...] = acc

    @pl.when(c == pl.num_programs(1) - 1)
    def _():
        for h in range(SB_HEADS):
            o_ref[:, h * SB_HEAD_DIM:(h + 1) * SB_HEAD_DIM] = acc[h * t:(h + 1) * t]


def _pages_per_step(n_pages):
    return max(p for p in range(1, 9) if n_pages % p == 0)


def _sb_sample(proj, cache_k, cache_v, j, page_table, batch, t):
    n_pool, page = cache_k.shape[1], cache_k.shape[2]
    assert page == LANES
    n_pages = page_table.shape[1]
    npp = _pages_per_step(n_pages)
    ck = cache_k.reshape(-1, page * SB_KV_HEADS, SB_HEAD_DIM)
    cv = cache_v.reshape(-1, page * SB_KV_HEADS, SB_HEAD_DIM)
    pt = (page_table + j * n_pool).reshape(-1)
    kcol = SB_WIDTH // SB_KV_WIDTH

    def page_spec(p):
        return pl.BlockSpec((1, page * SB_KV_HEADS, SB_HEAD_DIM),
                            lambda b, c, pt: (pt[b * n_pages + n_pages - 1 - (c * npp + p)], 0, 0))

    grid_spec = pltpu.PrefetchScalarGridSpec(
        num_scalar_prefetch=1,
        grid=(batch, n_pages // npp),
        in_specs=[
            pl.BlockSpec((t, SB_WIDTH), lambda b, c, pt: (b, 0)),
            pl.BlockSpec((t, SB_KV_WIDTH), lambda b, c, pt: (b, kcol)),
            pl.BlockSpec((t, SB_KV_WIDTH), lambda b, c, pt: (b, kcol + 1)),
        ] + [page_spec(p) for p in range(npp)] * 2,
        out_specs=pl.BlockSpec((t, SB_WIDTH), lambda b, c, pt: (b, 0)),
        scratch_shapes=[pltpu.VMEM((SB_HEADS * t, 1), F32), pltpu.VMEM((SB_HEADS * t, SB_HEAD_DIM), F32)],
    )
    return pl.pallas_call(
        functools.partial(_sb_sample_kernel, t=t, npp=npp),
        grid_spec=grid_spec,
        out_shape=jax.ShapeDtypeStruct((batch * t, SB_WIDTH), F32),
        compiler_params=_cparams(("parallel", "arbitrary")),
        name="sb_sample",
    )(pt, proj, proj, proj, *([ck] * npp), *([cv] * npp))


def _rope_tables(pos):
    half = MLA_ROPE // 2
    inv_freq = ROPE_THETA ** (-jnp.arange(half, dtype=F32) / half)
    ang = pos.astype(F32)[:, None] * inv_freq[None, :]
    cos, sin = jnp.cos(ang), jnp.sin(ang)
    zero = jnp.zeros_like(sin)
    reps = LANES // MLA_ROPE
    return (jnp.tile(jnp.concatenate([cos, cos], axis=1), (1, reps)),
            jnp.tile(jnp.concatenate([-sin, zero], axis=1), (1, reps)),
            jnp.tile(jnp.concatenate([zero, sin], axis=1), (1, reps)))


def _rope128(x, c, s_lo, s_hi):
    half = MLA_ROPE // 2
    return x * c + pltpu.roll(x, LANES - half, 1) * s_lo + pltpu.roll(x, half, 1) * s_hi


def _mla_prep_kernel(q_ref, ckv_ref, kr_ref, kvn_ref, wuk_ref, c_ref, slo_ref, shi_ref,
                     qlat_ref, qrope_ref, ckvn_ref, krr_ref):
    c, slo, shi = c_ref[...], slo_ref[...], shi_ref[...]
    for h in range(MLA_HEADS):
        qn = q_ref[:, h * MLA_NOPE:(h + 1) * MLA_NOPE].astype(BF16)
        qlat_ref[:, h * MLA_KV_LORA:(h + 1) * MLA_KV_LORA] = _dot(qn, wuk_ref[h])
    rope0 = MLA_HEADS * MLA_NOPE
    for i in range(MLA_HEADS * MLA_ROPE // LANES):
        qrope_ref[:, i * LANES:(i + 1) * LANES] = _rope128(
            q_ref[:, rope0 + i * LANES:rope0 + (i + 1) * LANES], c, slo, shi)
    ckv = ckv_ref[...]
    ms = jnp.mean(ckv * ckv, axis=-1, keepdims=True)
    ckvn_ref[...] = ckv * lax.rsqrt(ms + NORM_EPS) * kvn_ref[...]
    krr_ref[...] = _rope128(kr_ref[:, :LANES], c, slo, shi)[:, :MLA_ROPE]


def _mla_prep(q, proj, kv_norm, wuk_t, tabs):
    m = q.shape[0]
    tm = min(m, 512)
    qw = MLA_HEADS * (MLA_NOPE + MLA_ROPE)
    row = lambda w: pl.BlockSpec((tm, w), lambda i: (i, 0))
    return pl.pallas_call(
        _mla_prep_kernel,
        grid=(m // tm,),
        in_specs=[
            row(qw),
            pl.BlockSpec((tm, MLA_KV_LORA), lambda i: (i, MLA_Q_LORA // MLA_KV_LORA)),
            pl.BlockSpec((tm, MLA_KV_LORA), lambda i: (i, MLA_Q_LORA // MLA_KV_LORA + 1)),
            pl.BlockSpec((1, MLA_KV_LORA), lambda i: (0, 0)),
            pl.BlockSpec((MLA_HEADS, MLA_NOPE, MLA_KV_LORA), lambda i: (0, 0, 0)),
            row(LANES), row(LANES), row(LANES),
        ],
        out_specs=[row(MLA_HEADS * MLA_KV_LORA), row(MLA_HEADS * MLA_ROPE), row(MLA_KV_LORA), row(MLA_ROPE)],
        out_shape=[
            jax.ShapeDtypeStruct((m, MLA_HEADS * MLA_KV_LORA), F32),
            jax.ShapeDtypeStruct((m, MLA_HEADS * MLA_ROPE), F32),
            jax.ShapeDtypeStruct((m, MLA_KV_LORA), F32),
            jax.ShapeDtypeStruct((m, MLA_ROPE), F32),
        ],
        compiler_params=_cparams(("parallel",)),
        name="mla_prep",
    )(q, proj, proj, kv_norm.reshape(1, MLA_KV_LORA), wuk_t, *tabs)


MLA_QK = MLA_KV_LORA + LANES


def _mla_stack_q(qlat_ref, qrope_ref, t):
    pad = jnp.zeros((t, LANES - MLA_ROPE), F32)
    return jnp.concatenate(
        [jnp.concatenate([qlat_ref[:, h * MLA_KV_LORA:(h + 1) * MLA_KV_LORA],
                          qrope_ref[:, h * MLA_ROPE:(h + 1) * MLA_ROPE], pad], axis=1)
         for h in range(MLA_HEADS)], axis=0).astype(BF16)


def _mla_keys(ckv, kr, pad_rows=0):
    k = jnp.concatenate([ckv, kr, jnp.zeros((ckv.shape[0], LANES - MLA_ROPE), F32)], axis=1)
    if pad_rows:
        k = jnp.concatenate([k, jnp.zeros((pad_rows, MLA_QK), F32)], axis=0)
    return k.astype(BF16)


def _softmax_update(s, values, valid, m, l, acc):
    s = s * MLA_SCALE
    if valid is not None:
        s = jnp.where(valid, s, NEG_BIG)
    m_new = jnp.maximum(m, jnp.max(s, axis=-1, keepdims=True))
    corr = jnp.exp(m - m_new)
    p = jnp.exp(s - m_new)
    return (m_new, l * corr + jnp.sum(p, axis=-1, keepdims=True), acc * corr + _dot(p.astype(BF16), values))


def _mla_update(q, kcat, valid, m, l, acc):
    return _softmax_update(_dot_nt(q, kcat), kcat[:, :MLA_KV_LORA], valid, m, l, acc)


def _mla_finish(acc, l, wuv_ref, o_ref, t):
    o_lat = (acc / l).astype(BF16)
    for h in range(MLA_HEADS):
        o_ref[:, h * MLA_V:(h + 1) * MLA_V] = _dot(o_lat[h * t:(h + 1) * t], wuv_ref[h])


def _mla_prompt_kernel(qi_ref, kj_ref, first_ref, last_ref, qlat_ref, qrope_ref, ckv_ref, kr_ref, wuv_ref,
                       o_ref, q_ref, m_ref, l_ref, acc_ref, *, tq, tk, chunk):
    s = pl.program_id(1)
    rows = MLA_HEADS * tq

    @pl.when(first_ref[s] == 1)
    def _():
        q_ref[...] = _mla_stack_q(qlat_ref, qrope_ref, tq)
        m_ref[...] = jnp.full_like(m_ref, NEG_BIG)
        l_ref[...] = jnp.zeros_like(l_ref)
        acc_ref[...] = jnp.zeros_like(acc_ref)

    kcat = _mla_keys(ckv_ref[...], kr_ref[...])

    def step(valid):
        for c in range(rows // chunk):
            r = slice(c * chunk, (c + 1) * chunk)
            m, l, acc = _mla_update(q_ref[r], kcat, valid, m_ref[r], l_ref[r], acc_ref[r])
            m_ref[r] = m
            l_ref[r] = l
            acc_ref[r] = acc

    crosses = (kj_ref[s] + 1) * tk - 1 > qi_ref[s] * tq

    @pl.when(crosses)
    def _():
        qpos = qi_ref[s] * tq + _row_mod((chunk, tk), tq)
        kpos = kj_ref[s] * tk + lax.broadcasted_iota(jnp.int32, (chunk, tk), 1)
        step(kpos <= qpos)

    @pl.when(jnp.logical_not(crosses))
    def _():
        step(None)

    @pl.when(last_ref[s] == 1)
    def _():
        _mla_finish(acc_ref[...], l_ref[...], wuv_ref, o_ref, tq)


def _mla_prompt(qlat, qrope, ckvn, krr, wuv_t, batch, seq):
    tq = min(seq, 64)
    tk = min(seq, 512)
    tabs = _causal_steps(seq, tq, tk, descending=False)
    nsteps = tabs[0].shape[0]
    nq, nkb = seq // tq, seq // tk
    rows = MLA_HEADS * tq
    grid_spec = pltpu.PrefetchScalarGridSpec(
        num_scalar_prefetch=4,
        grid=(batch, nsteps),
        in_specs=[
            pl.BlockSpec((tq, MLA_HEADS * MLA_KV_LORA), lambda b, s, qi, kj, f, l: (b * nq + qi[s], 0)),
            pl.BlockSpec((tq, MLA_HEADS * MLA_ROPE), lambda b, s, qi, kj, f, l: (b * nq + qi[s], 0)),
            pl.BlockSpec((tk, MLA_KV_LORA), lambda b, s, qi, kj, f, l: (b * nkb + kj[s], 0)),
            pl.BlockSpec((tk, MLA_ROPE), lambda b, s, qi, kj, f, l: (b * nkb + kj[s], 0)),
            pl.BlockSpec((MLA_HEADS, MLA_KV_LORA, MLA_V), lambda b, s, qi, kj, f, l: (0, 0, 0)),
        ],
        out_specs=pl.BlockSpec((tq, MLA_WIDTH), lambda b, s, qi, kj, f, l: (b * nq + qi[s], 0)),
        scratch_shapes=[pltpu.VMEM((rows, MLA_QK), BF16),
                        pltpu.VMEM((rows, 1), F32), pltpu.VMEM((rows, 1), F32),
                        pltpu.VMEM((rows, MLA_KV_LORA), F32)],
    )
    chunk = min(rows, 4 * tq)
    return pl.pallas_call(
        functools.partial(_mla_prompt_kernel, tq=tq, tk=tk, chunk=chunk),
        grid_spec=grid_spec,
        out_shape=jax.ShapeDtypeStruct((batch * seq, MLA_WIDTH), F32),
        compiler_params=_cparams(("parallel", "arbitrary")),
        name="mla_prompt",
    )(*tabs, qlat, qrope, ckvn, krr, wuv_t)


def _mla_sample_kernel(pt_ref, qlat_ref, qrope_ref, cn_ref, kn_ref, wuv_ref, *refs, t, npp):
    c_refs, k_refs = refs[:npp], refs[npp:2 * npp]
    o_ref, m_ref, l_ref, acc_ref = refs[2 * npp:]
    c = pl.program_id(1)
    rows = MLA_HEADS * t
    q = _mla_stack_q(qlat_ref, qrope_ref, t)

    @pl.when(c == 0)
    def _():
        tq = _row_mod((rows, LANES), t)
        kidx = lax.broadcasted_iota(jnp.int32, (rows, LANES), 1)
        m, l, acc = _mla_update(q, _mla_keys(cn_ref[...], kn_ref[...], LANES - t), kidx <= tq,
                                jnp.full((rows, 1), NEG_BIG, F32), jnp.zeros((rows, 1), F32),
                                jnp.zeros((rows, MLA_KV_LORA), F32))
        m_ref[...] = m
        l_ref[...] = l
        acc_ref[...] = acc

    ckv = jnp.concatenate([r[0] for r in c_refs], axis=0).astype(BF16)
    kr_t = jnp.concatenate([r[0] for r in k_refs], axis=1)
    kr_t = jnp.concatenate([kr_t, jnp.zeros((LANES - MLA_ROPE, kr_t.shape[1]), F32)], axis=0).astype(BF16)
    s = _dot_nt(q[:, :MLA_KV_LORA], ckv) + _dot(q[:, MLA_KV_LORA:], kr_t)
    m, l, acc = _softmax_update(s, ckv, None, m_ref[...], l_ref[...], acc_ref[...])
    m_ref[...] = m
    l_ref[...] = l
    acc_ref[...] = acc

    @pl.when(c == pl.num_programs(1) - 1)
    def _():
        _mla_finish(acc, l, wuv_ref, o_ref, t)


def _mla_sample(qlat, qrope, ckvn, krr, wuv_t, cache_ckv, cache_kr, j, page_table, batch, t):
    n_pool, page = cache_ckv.shape[1], cache_ckv.shape[2]
    assert page == LANES
    n_pages = page_table.shape[1]
    npp = _pages_per_step(n_pages)
    cc = cache_ckv.reshape(-1, page, MLA_KV_LORA)
    ck = jnp.swapaxes(cache_kr, 2, 3).reshape(-1, MLA_ROPE, page)
    pt = (page_table + j * n_pool).reshape(-1)
    rows = MLA_HEADS * t

    def page_spec(p, shape):
        return pl.BlockSpec((1,) + shape, lambda b, c, pt: (pt[b * n_pages + c * npp + p], 0, 0))

    grid_spec = pltpu.PrefetchScalarGridSpec(
        num_scalar_prefetch=1,
        grid=(batch, n_pages // npp),
        in_specs=[
            pl.BlockSpec((t, MLA_HEADS * MLA_KV_LORA), lambda b, c, pt: (b, 0)),
            pl.BlockSpec((t, MLA_HEADS * MLA_ROPE), lambda b, c, pt: (b, 0)),
            pl.BlockSpec((t, MLA_KV_LORA), lambda b, c, pt: (b, 0)),
            pl.BlockSpec((t, MLA_ROPE), lambda b, c, pt: (b, 0)),
            pl.BlockSpec((MLA_HEADS, MLA_KV_LORA, MLA_V), lambda b, c, pt: (0, 0, 0)),
        ] + [page_spec(p, (page, MLA_KV_LORA)) for p in range(npp)]
          + [page_spec(p, (MLA_ROPE, page)) for p in range(npp)],
        out_specs=pl.BlockSpec((t, MLA_WIDTH), lambda b, c, pt: (b, 0)),
        scratch_shapes=[pltpu.VMEM((rows, 1), F32), pltpu.VMEM((rows, 1), F32),
                        pltpu.VMEM((rows, MLA_KV_LORA), F32)],
    )
    return pl.pallas_call(
        functools.partial(_mla_sample_kernel, t=t, npp=npp),
        grid_spec=grid_spec,
        out_shape=jax.ShapeDtypeStruct((batch * t, MLA_WIDTH), F32),
        compiler_params=_cparams(("parallel", "arbitrary")),
        name="mla_sample",
    )(pt, qlat, qrope, ckvn, krr, wuv_t, *([cc] * npp), *([ck] * npp))


def kernel(x_prompt, x_sample, state_ssd_conv, state_ssd_h, cache_sb_k, cache_sb_v, cache_mla_ckv,
           cache_mla_krope, page_table, norm_pre, norm_post, ssd_w_in, ssd_conv_w, ssd_conv_b, ssd_dt_bias,
           ssd_a_log, ssd_d, ssd_norm_w, ssd_w_out, sb_w_in, sb_w_out, mla_w_in, mla_q_norm, mla_kv_norm,
           mla_w_uq, mla_w_uk, mla_w_uv, mla_w_out):
    bp, seq, d = x_prompt.shape
    bs, t, _ = x_sample.shape
    depth = norm_pre.shape[0]
    past = page_table.shape[1] * cache_sb_k.shape[2]
    assert seq % SSD_CHUNK == 0 and t >= SSD_CONV - 1
    xp = x_prompt.reshape(bp * seq, d)
    xs = x_sample.reshape(bs * t, d)
    outs = {k: [] for k in ("p_conv", "p_h", "p_k", "p_v", "p_ckv", "p_kr",
                            "s_conv", "s_h", "s_k", "s_v", "s_ckv", "s_kr")}
    zx = SSD_D_INNER + SSD_CONV_DIM
    for i in range(depth):
        kind, j = i % N_MIXERS, i // N_MIXERS
        if kind == 0:
            w_zx = ssd_w_in[j][:, :zx].astype(BF16)
            w_dt = jnp.pad(ssd_w_in[j][:, zx:], ((0, 0), (0, LANES - SSD_HEADS))).astype(BF16)
            w_out = ssd_w_out[j].astype(BF16)
            consts = _ssd_consts(ssd_conv_w[j], ssd_conv_b[j], ssd_dt_bias[j], ssd_a_log[j], ssd_d[j],
                                 ssd_norm_w[j])
            proj_p = _norm_matmul(xp, norm_pre[i], w_zx, name="ssd_in_p")
            proj_s = _norm_matmul(xs, norm_pre[i], w_zx, name="ssd_in_s")
            dt_p = _norm_matmul(xp, norm_pre[i], w_dt, name="ssd_dt_p")
            dt_s = _norm_matmul(xs, norm_pre[i], w_dt, name="ssd_dt_s")
            y_p, h_p = _ssd_prompt(proj_p, dt_p, consts, bp, seq)
            h0 = state_ssd_h.reshape(-1, SSD_D_INNER, SSD_STATE)
            y_s, h_s = _ssd_sample(proj_s, dt_s, state_ssd_conv[j], h0, j, consts, bs, t)
            outs["p_conv"].append(proj_p.reshape(bp, seq, zx)[:, seq - (SSD_CONV - 1):, SSD_D_INNER:])
            outs["s_conv"].append(proj_s.reshape(bs, t, zx)[:, t - (SSD_CONV - 1):, SSD_D_INNER:])
            outs["p_h"].append(h_p.reshape(bp, SSD_HEADS, SSD_HEAD_DIM, SSD_STATE))
            outs["s_h"].append(h_s.reshape(bs, SSD_HEADS, SSD_HEAD_DIM, SSD_STATE))
            xp = _matmul_post(y_p, w_out, xp, norm_post[i], name="ssd_out_p")
            xs = _matmul_post(y_s, w_out, xs, norm_post[i], name="ssd_out_s")
        elif kind == 1:
            w_in = sb_w_in[j].astype(BF16)
            w_out = sb_w_out[j].astype(BF16)
            proj_p = _norm_matmul(xp, norm_pre[i], w_in, name="sb_in_p")
            proj_s = _norm_matmul(xs, norm_pre[i], w_in, name="sb_in_s")
            gcol = SB_WIDTH + 2 * SB_KV_WIDTH
            a_p = _sb_prompt(proj_p, bp, seq)
            a_s = _sb_sample(proj_s, cache_sb_k, cache_sb_v, j, page_table, bs, t)
            for tag, pr, nb, ln in (("p", proj_p, bp, seq), ("s", proj_s, bs, t)):
                kv = pr.reshape(nb, ln, -1)
                outs[tag + "_k"].append(kv[:, :, SB_WIDTH:SB_WIDTH + SB_KV_WIDTH].reshape(nb, ln, SB_KV_HEADS, SB_HEAD_DIM))
                outs[tag + "_v"].append(kv[:, :, SB_WIDTH + SB_KV_WIDTH:gcol].reshape(nb, ln, SB_KV_HEADS, SB_HEAD_DIM))
            xp = _matmul_post(a_p, w_out, xp, norm_post[i], gate=proj_p, gate_col=gcol, name="sb_out_p")
            xs = _matmul_post(a_s, w_out, xs, norm_post[i], gate=proj_s, gate_col=gcol, name="sb_out_s")
        else:
            o1, o2, o3 = MLA_Q_LORA, MLA_Q_LORA + MLA_KV_LORA, MLA_Q_LORA + MLA_KV_LORA + MLA_ROPE
            w = mla_w_in[j]
            w_in = jnp.concatenate([w[:, :o3], jnp.zeros((d, MLA_KV_LORA - MLA_ROPE), F32), w[:, o3:]],
                                   axis=1).astype(BF16)
            gcol = o2 + MLA_KV_LORA
            uq = mla_w_uq[j]
            w_uq = jnp.concatenate([uq[:, :, :MLA_NOPE].reshape(MLA_Q_LORA, -1),
                                    uq[:, :, MLA_NOPE:].reshape(MLA_Q_LORA, -1)], axis=1).astype(BF16)
            wuk_t = jnp.transpose(mla_w_uk[j], (1, 2, 0)).astype(BF16)
            wuv_t = jnp.transpose(mla_w_uv[j], (1, 0, 2)).astype(BF16)
            w_out = mla_w_out[j].astype(BF16)
            res = []
            for tag, xx, nb, ln, pos in (("p", xp, bp, seq, jnp.arange(seq)), ("s", xs, bs, t, past + jnp.arange(t))):
                proj = _norm_matmul(xx, norm_pre[i], w_in, name="mla_in_" + tag)
                q = _norm_matmul(proj, mla_q_norm[j], w_uq, name="mla_uq_" + tag)
                tabs = tuple(jnp.tile(tb, (nb, 1)) for tb in _rope_tables(pos))
                qlat, qrope, ckvn, krr = _mla_prep(q, proj, mla_kv_norm[j], wuk_t, tabs)
                outs[tag + "_ckv"].append(ckvn.reshape(nb, ln, MLA_KV_LORA))
                outs[tag + "_kr"].append(krr.reshape(nb, ln, MLA_ROPE))
                if tag == "p":
                    o = _mla_prompt(qlat, qrope, ckvn, krr, wuv_t, nb, ln)
                else:
                    o = _mla_sample(qlat, qrope, ckvn, krr, wuv_t, cache_mla_ckv, cache_mla_krope, j,
                                    page_table, nb, ln)
                res.append(_matmul_post(o, w_out, xx, norm_post[i], gate=proj, gate_col=gcol,
                                        name="mla_out_" + tag))
            xp, xs = res
    st = lambda k: jnp.stack(outs[k])
    return (xp.reshape(bp, seq, d), xs.reshape(bs, t, d),
            st("p_conv"), st("p_h"), st("p_k"), st("p_v"), st("p_ckv"), st("p_kr"),
            st("s_conv"), st("s_h"), st("s_k"), st("s_v"), st("s_ckv"), st("s_kr"))
```

```python
import functools
import math

import numpy as np
import jax
import jax.numpy as jnp
from jax import lax
from jax.experimental import pallas as pl
from jax.experimental.pallas import tpu as pltpu

F32 = jnp.float32
BF16 = jnp.bfloat16

D_MODEL = 2048
N_MIXERS = 3
NORM_EPS = 1e-6

SSD_D_INNER = 2 * D_MODEL
SSD_HEAD_DIM = 64
SSD_HEADS = SSD_D_INNER // SSD_HEAD_DIM
SSD_GROUPS = 8
SSD_STATE = 128
SSD_CONV = 4
SSD_CHUNK = 128
SSD_BC = SSD_GROUPS * SSD_STATE
SSD_CONV_DIM = SSD_D_INNER + 2 * SSD_BC
SSD_GROUP_WIDTH = SSD_D_INNER // SSD_GROUPS
SSD_HEADS_PER_GROUP = SSD_HEADS // SSD_GROUPS

SB_HEADS = 16
SB_KV_HEADS = 4
SB_GROUP = SB_HEADS // SB_KV_HEADS
SB_HEAD_DIM = 128
SB_WIDTH = SB_HEADS * SB_HEAD_DIM
SB_KV_WIDTH = SB_KV_HEADS * SB_HEAD_DIM
SB_SCALE = SB_HEAD_DIM ** -0.5

MLA_HEADS = 16
MLA_Q_LORA = 512
MLA_KV_LORA = 256
MLA_NOPE = 128
MLA_ROPE = 64
MLA_V = 128
MLA_WIDTH = MLA_HEADS * MLA_V
MLA_SCALE = (MLA_NOPE + MLA_ROPE) ** -0.5
ROPE_THETA = 10000.0

LANES = 128
SUBLANES = 8
VMEM_LIMIT_BYTES = 48 * 1024 * 1024
NEG_BIG = -1e30


def _cparams(sem):
    return pltpu.CompilerParams(dimension_semantics=sem, vmem_limit_bytes=VMEM_LIMIT_BYTES)


def _dot(a, b):
    return jnp.dot(a, b, preferred_element_type=F32)


def _dot_nt(a, b):
    return lax.dot_general(a, b, (((1,), (1,)), ((), ())), preferred_element_type=F32)


def _dot_tn(a, b):
    return lax.dot_general(a, b, (((0,), (0,)), ((), ())), preferred_element_type=F32)


def _split_bf16(a, terms):
    parts, r = [], a
    for i in range(terms):
        p = r.astype(BF16)
        parts.append(p)
        if i + 1 < terms:
            r = r - p.astype(F32)
    return parts


def _dot_sel_rhs(a, sel, terms):
    out = None
    for p in _split_bf16(a, terms):
        d = _dot(p, sel)
        out = d if out is None else out + d
    return out


def _dot_sel_lhs(sel, a, terms):
    out = None
    for p in _split_bf16(a, terms):
        d = _dot(sel, p)
        out = d if out is None else out + d
    return out


def _row_mod(shape, n):
    assert n & (n - 1) == 0
    return lax.broadcasted_iota(jnp.int32, shape, 0) & (n - 1)


def _softplus(x):
    return jnp.maximum(x, 0.0) + jnp.log1p(jnp.exp(-jnp.abs(x)))


def _silu(x):
    return x * jax.nn.sigmoid(x)


def _norm_matmul_kernel(x_ref, nw_ref, w_ref, o_ref, xn_ref):
    @pl.when(pl.program_id(1) == 0)
    def _():
        x = x_ref[...]
        ms = jnp.mean(x * x, axis=-1, keepdims=True)
        xn_ref[...] = (x * lax.rsqrt(ms + NORM_EPS) * nw_ref[...]).astype(BF16)

    o_ref[...] = _dot(xn_ref[...], w_ref[...])


def _norm_matmul(x, nw, w, *, col_block=0, name):
    m = x.shape[0]
    k, n = w.shape
    tm = min(m, 1024)
    tn = min(n, 512)
    return pl.pallas_call(
        _norm_matmul_kernel,
        grid=(m // tm, n // tn),
        in_specs=[
            pl.BlockSpec((tm, k), lambda i, j: (i, col_block)),
            pl.BlockSpec((1, k), lambda i, j: (0, 0)),
            pl.BlockSpec((k, tn), lambda i, j: (0, j)),
        ],
        out_specs=pl.BlockSpec((tm, tn), lambda i, j: (i, j)),
        out_shape=jax.ShapeDtypeStruct((m, n), F32),
        scratch_shapes=[pltpu.VMEM((tm, k), BF16)],
        compiler_params=_cparams(("parallel", "arbitrary")),
        name=name,
    )(x, nw.reshape(1, k), w)


def _post_kernel(*refs, gated, nk):
    if gated:
        a_ref, g_ref, w_ref, res_ref, pw_ref, o_ref, acc_ref = refs
    else:
        a_ref, w_ref, res_ref, pw_ref, o_ref, acc_ref = refs
    kk = pl.program_id(1)
    a = a_ref[...].astype(F32)
    if gated:
        a = a * _silu(g_ref[...])
    part = _dot(a.astype(BF16), w_ref[...])

    @pl.when(kk == 0)
    def _():
        acc_ref[...] = part

    @pl.when(kk > 0)
    def _():
        acc_ref[...] += part

    @pl.when(kk == nk - 1)
    def _():
        y = acc_ref[...]
        ms = jnp.mean(y * y, axis=-1, keepdims=True)
        o_ref[...] = res_ref[...] + y * lax.rsqrt(ms + NORM_EPS) * pw_ref[...]


def _matmul_post(a, w, res, pw, *, gate=None, gate_col=0, name):
    m, k = a.shape
    n = w.shape[1]
    tm = min(m, 512)
    tk = min(k, 512)
    nk = k // tk
    gated = gate is not None
    in_specs = [pl.BlockSpec((tm, tk), lambda i, kk: (i, kk))]
    args = [a]
    if gated:
        gb = gate_col // tk
        in_specs.append(pl.BlockSpec((tm, tk), lambda i, kk: (i, gb + kk)))
        args.append(gate)
    in_specs += [
        pl.BlockSpec((tk, n), lambda i, kk: (kk, 0)),
        pl.BlockSpec((tm, n), lambda i, kk: (i, 0)),
        pl.BlockSpec((1, n), lambda i, kk: (0, 0)),
    ]
    args += [w, res, pw.reshape(1, n)]
    return pl.pallas_call(
        functools.partial(_post_kernel, gated=gated, nk=nk),
        grid=(m // tm, nk),
        in_specs=in_specs,
        out_specs=pl.BlockSpec((tm, n), lambda i, kk: (i, 0)),
        out_shape=jax.ShapeDtypeStruct((m, n), F32),
        scratch_shapes=[pltpu.VMEM((tm, n), F32)],
        compiler_params=_cparams(("parallel", "arbitrary")),
        name=name,
    )(*args)


def _conv_taps(ext, cur, w, bias, rows):
    off = SUBLANES - (SSD_CONV - 1)
    acc = ext[off:off + rows] * w[0:1]
    for k in range(1, SSD_CONV - 1):
        acc = acc + ext[off + k:off + k + rows] * w[k:k + 1]
    acc = acc + cur * w[SSD_CONV - 1:SSD_CONV]
    return _silu(acc + bias)


def _gate_norm(y, z, nw):
    y = y * _silu(z)
    ms = jnp.mean(y * y, axis=-1, keepdims=True)
    return y * lax.rsqrt(ms + NORM_EPS) * nw


def _ssd_prompt_kernel(z_ref, xs_ref, bm_ref, cm_ref, dt_ref, cw_ref, cb_ref, dtb_ref, alog_ref,
                       dsk_ref, nw_ref, y_ref, h_ref, tail_ref):
    cs = SSD_CHUNK

    @pl.when(pl.program_id(1) == 0)
    def _():
        h_ref[...] = jnp.zeros_like(h_ref)
        tail_ref[...] = jnp.zeros_like(tail_ref)

    def conv(cur, lo, width):
        ext = jnp.concatenate([tail_ref[:, lo:lo + width], cur], axis=0)
        out = _conv_taps(ext, cur, cw_ref[:, lo:lo + width], cb_ref[:, lo:lo + width], cs)
        tail_ref[:, lo:lo + width] = cur[cs - SUBLANES:cs]
        return out

    row = lax.broadcasted_iota(jnp.int32, (cs, cs), 0)
    col = lax.broadcasted_iota(jnp.int32, (cs, cs), 1)
    causal = row >= col
    tril = causal.astype(BF16)
    left = col < SSD_HEAD_DIM

    dt = _softplus(dt_ref[...] + dtb_ref[...])
    la = dt * (-jnp.exp(alog_ref[...]))
    lc = _dot_sel_lhs(tril, la, 3)
    lc_t = lc.T
    dt_t = dt.T
    e_all = jnp.exp(lc)
    lc_last = lc[cs - 1:cs, :]
    de_all = jnp.exp(lc_last - lc) * dt
    cd_all = jnp.exp(lc_last)

    b_act = conv(bm_ref[...], SSD_D_INNER, SSD_BC)
    c_act = conv(cm_ref[...], SSD_D_INNER + SSD_BC, SSD_BC)

    for g in range(SSD_GROUPS):
        gl = g * SSD_GROUP_WIDTH
        xg = conv(xs_ref[:, gl:gl + SSD_GROUP_WIDTH], gl, SSD_GROUP_WIDTH)
        bg = b_act[:, g * SSD_STATE:(g + 1) * SSD_STATE].astype(BF16)
        cg = c_act[:, g * SSD_STATE:(g + 1) * SSD_STATE].astype(BF16)
        cbm = _dot_nt(cg, bg)
        h_prev = h_ref[0, gl:gl + SSD_GROUP_WIDTH, :]
        y_off = _dot_nt(cg, h_prev.astype(BF16))
        y_parts, xw_parts, cd_parts = [], [], []
        for pr in range(SSD_HEADS_PER_GROUP // 2):
            xp = xg[:, pr * LANES:(pr + 1) * LANES]
            ws, es, des = [], [], []
            for hh in (0, 1):
                h = g * SSD_HEADS_PER_GROUP + 2 * pr + hh
                seg = lc[:, h:h + 1] - lc_t[h:h + 1, :]
                decay = jnp.where(causal, jnp.exp(jnp.where(causal, seg, 0.0)), 0.0)
                ws.append(cbm * decay * dt_t[h:h + 1, :])
                es.append(jnp.broadcast_to(e_all[:, h:h + 1], (cs, LANES)))
                des.append(jnp.broadcast_to(de_all[:, h:h + 1], (cs, LANES)))
                cd_parts.append(jnp.broadcast_to(cd_all[:, h:h + 1], (SSD_HEAD_DIM, SSD_STATE)))
            w2 = jnp.concatenate(ws, axis=1).astype(BF16)
            x2 = jnp.concatenate([jnp.where(left, xp, 0.0), jnp.where(left, 0.0, xp)],
                                 axis=0).astype(BF16)
            y_diag = _dot(w2, x2)
            e2 = jnp.where(left, es[0], es[1])
            y_parts.append(y_diag + y_off[:, pr * LANES:(pr + 1) * LANES] * e2
                           + dsk_ref[:, gl + pr * LANES:gl + (pr + 1) * LANES] * xp)
            xw_parts.append(xp * jnp.where(left, des[0], des[1]))
        yg = jnp.concatenate(y_parts, axis=1)
        y_ref[:, gl:gl + SSD_GROUP_WIDTH] = _gate_norm(
            yg, z_ref[:, gl:gl + SSD_GROUP_WIDTH], nw_ref[:, gl:gl + SSD_GROUP_WIDTH]).astype(y_ref.dtype)
        xw = jnp.concatenate(xw_parts, axis=1).astype(BF16)
        states = _dot_tn(xw, bg)
        h_ref[0, gl:gl + SSD_GROUP_WIDTH, :] = h_prev * jnp.concatenate(cd_parts, axis=0) + states


def _ssd_consts(conv_w, conv_b, dt_bias, a_log, d_skip, norm_w):
    pad = LANES - SSD_HEADS
    return (conv_w, conv_b.reshape(1, SSD_CONV_DIM),
            jnp.pad(dt_bias, (0, pad)).reshape(1, LANES), jnp.pad(a_log, (0, pad)).reshape(1, LANES),
            jnp.repeat(d_skip, SSD_HEAD_DIM).reshape(1, SSD_D_INNER), norm_w.reshape(1, SSD_D_INNER))


def _ssd_prompt(proj, dtraw, consts, batch, seq):
    cs = SSD_CHUNK
    nc = seq // cs
    zb = SSD_D_INNER // SSD_BC
    const_specs = [
        pl.BlockSpec((SSD_CONV, SSD_CONV_DIM), lambda b, c: (0, 0)),
        pl.BlockSpec((1, SSD_CONV_DIM), lambda b, c: (0, 0)),
        pl.BlockSpec((1, LANES), lambda b, c: (0, 0)),
        pl.BlockSpec((1, LANES), lambda b, c: (0, 0)),
        pl.BlockSpec((1, SSD_D_INNER), lambda b, c: (0, 0)),
        pl.BlockSpec((1, SSD_D_INNER), lambda b, c: (0, 0)),
    ]
    return pl.pallas_call(
        _ssd_prompt_kernel,
        grid=(batch, nc),
        in_specs=[
            pl.BlockSpec((cs, SSD_D_INNER), lambda b, c: (b * nc + c, 0)),
            pl.BlockSpec((cs, SSD_D_INNER), lambda b, c: (b * nc + c, 1)),
            pl.BlockSpec((cs, SSD_BC), lambda b, c: (b * nc + c, 2 * zb)),
            pl.BlockSpec((cs, SSD_BC), lambda b, c: (b * nc + c, 2 * zb + 1)),
            pl.BlockSpec((cs, LANES), lambda b, c: (b * nc + c, 0)),
        ] + const_specs,
        out_specs=[
            pl.BlockSpec((cs, SSD_D_INNER), lambda b, c: (b * nc + c, 0)),
            pl.BlockSpec((1, SSD_D_INNER, SSD_STATE), lambda b, c: (b, 0, 0)),
        ],
        out_shape=[
            jax.ShapeDtypeStruct((batch * seq, SSD_D_INNER), BF16),
            jax.ShapeDtypeStruct((batch, SSD_D_INNER, SSD_STATE), F32),
        ],
        scratch_shapes=[pltpu.VMEM((SUBLANES, SSD_CONV_DIM), F32)],
        compiler_params=_cparams(("parallel", "arbitrary")),
        name="ssd_prompt",
    )(proj, proj, proj, proj, dtraw, *consts)


def _ssd_sample_kernel(z_ref, xs_ref, bm_ref, cm_ref, dt_ref, st_ref, h0_ref, cw_ref, cb_ref, dtb_ref,
                       alog_ref, dsk_ref, nw_ref, rexp_ref, gsel_ref, y_ref, h_ref):
    t = xs_ref.shape[0]
    pairs = t * t

    def conv(cur, lo, width):
        ext = jnp.concatenate([st_ref[0, :, lo:lo + width], cur], axis=0)
        return _conv_taps(ext, cur, cw_ref[:, lo:lo + width], cb_ref[:, lo:lo + width], t)

    def tile_t(a):
        return jnp.concatenate([a] * t, axis=0)

    def tile_s(a):
        return jnp.concatenate([jnp.broadcast_to(a[s:s + 1], a.shape) for s in range(t)], axis=0)

    x_act = conv(xs_ref[...], 0, SSD_D_INNER)
    b_act = conv(bm_ref[...], SSD_D_INNER, SSD_BC)
    c_act = conv(cm_ref[...], SSD_D_INNER + SSD_BC, SSD_BC)

    dt = _softplus(dt_ref[...] + dtb_ref[...])
    la = dt * (-jnp.exp(alog_ref[...]))
    rix = lax.broadcasted_iota(jnp.int32, (t, LANES), 0)
    lc = la
    sh = 1
    while sh < t:
        lc = lc + jnp.where(rix >= sh, pltpu.roll(lc, sh, 0), 0.0)
        sh *= 2
    lc_last = lc[t - 1:t, :]
    rexp = rexp_ref[...]

    r = lax.broadcasted_iota(jnp.int32, (pairs, LANES), 0)
    ok = lax.shift_right_logical(r, int(math.log2(t))) <= (r & (t - 1))
    decay = jnp.where(ok, jnp.exp(jnp.where(ok, tile_t(lc) - tile_s(lc), 0.0)), 0.0)
    cbe = _dot_sel_rhs(tile_t(c_act) * tile_s(b_act), gsel_ref[...], 3)
    coef = _dot_sel_rhs(cbe * decay * tile_s(dt), rexp, 3)
    y = dsk_ref[...] * x_act
    for s in range(t):
        y = y + coef[s * t:(s + 1) * t] * jnp.broadcast_to(x_act[s:s + 1], x_act.shape)

    e_exp = _dot_sel_rhs(jnp.exp(lc), rexp, 3)
    xw = x_act * _dot_sel_rhs(jnp.exp(lc_last - lc) * dt, rexp, 3)
    cd_all = jnp.exp(lc_last)
    zero_rows = jnp.zeros((LANES - t, SSD_GROUP_WIDTH), F32)
    zero_b = jnp.zeros((LANES - t, SSD_STATE), F32)
    for g in range(SSD_GROUPS):
        gl = g * SSD_GROUP_WIDTH
        bg = b_act[:, g * SSD_STATE:(g + 1) * SSD_STATE]
        cg = c_act[:, g * SSD_STATE:(g + 1) * SSD_STATE]
        h_prev = h0_ref[0, gl:gl + SSD_GROUP_WIDTH, :]
        cg16 = jnp.concatenate([cg, jnp.zeros_like(cg)], axis=0).astype(BF16)
        y_off = _dot_nt(cg16, h_prev.astype(BF16))[:t]
        yg = y[:, gl:gl + SSD_GROUP_WIDTH] + y_off * e_exp[:, gl:gl + SSD_GROUP_WIDTH]
        y_ref[:, gl:gl + SSD_GROUP_WIDTH] = _gate_norm(
            yg, z_ref[:, gl:gl + SSD_GROUP_WIDTH], nw_ref[:, gl:gl + SSD_GROUP_WIDTH]).astype(y_ref.dtype)
        xw_pad = jnp.concatenate([xw[:, gl:gl + SSD_GROUP_WIDTH], zero_rows], axis=0).astype(BF16)
        b_pad = jnp.concatenate([bg, zero_b], axis=0).astype(BF16)
        states = _dot_tn(xw_pad, b_pad)
        cd = jnp.concatenate(
            [jnp.broadcast_to(cd_all[:, h:h + 1], (SSD_HEAD_DIM, SSD_STATE))
             for h in range(g * SSD_HEADS_PER_GROUP, (g + 1) * SSD_HEADS_PER_GROUP)], axis=0)
        h_ref[0, gl:gl + SSD_GROUP_WIDTH, :] = h_prev * cd + states


def _ssd_sample(proj, dtraw, conv_state, h0, j, consts, batch, t):
    assert t == SUBLANES, "sample SSD path handles one sublane tile of new tokens"
    zb = SSD_D_INNER // SSD_BC
    hist = jnp.pad(conv_state, ((0, 0), (SUBLANES - (SSD_CONV - 1), 0), (0, 0)))
    heads = np.arange(LANES)
    rexp = (heads[:, None] == (np.arange(SSD_D_INNER) // SSD_HEAD_DIM)[None, :])
    gsel = ((np.arange(SSD_BC) // SSD_STATE)[:, None] == (heads // SSD_HEADS_PER_GROUP)[None, :])
    gsel = gsel & (heads < SSD_HEADS)[None, :]
    rexp = jnp.asarray(rexp, BF16)
    gsel = jnp.asarray(gsel, BF16)
    full = lambda shape: pl.BlockSpec(shape, lambda b: (0,) * len(shape))
    return pl.pallas_call(
        _ssd_sample_kernel,
        grid=(batch,),
        in_specs=[
            pl.BlockSpec((t, SSD_D_INNER), lambda b: (b, 0)),
            pl.BlockSpec((t, SSD_D_INNER), lambda b: (b, 1)),
            pl.BlockSpec((t, SSD_BC), lambda b: (b, 2 * zb)),
            pl.BlockSpec((t, SSD_BC), lambda b: (b, 2 * zb + 1)),
            pl.BlockSpec((t, LANES), lambda b: (b, 0)),
            pl.BlockSpec((1, SUBLANES, SSD_CONV_DIM), lambda b: (b, 0, 0)),
            pl.BlockSpec((1, SSD_D_INNER, SSD_STATE), lambda b: (j * batch + b, 0, 0)),
            full((SSD_CONV, SSD_CONV_DIM)), full((1, SSD_CONV_DIM)), full((1, LANES)), full((1, LANES)),
            full((1, SSD_D_INNER)), full((1, SSD_D_INNER)),
            full((LANES, SSD_D_INNER)), full((SSD_BC, LANES)),
        ],
        out_specs=[
            pl.BlockSpec((t, SSD_D_INNER), lambda b: (b, 0)),
            pl.BlockSpec((1, SSD_D_INNER, SSD_STATE), lambda b: (b, 0, 0)),
        ],
        out_shape=[
            jax.ShapeDtypeStruct((batch * t, SSD_D_INNER), F32),
            jax.ShapeDtypeStruct((batch, SSD_D_INNER, SSD_STATE), F32),
        ],
        compiler_params=_cparams(("parallel",)),
        name="ssd_sample",
    )(proj, proj, proj, proj, dtraw, hist, h0, *consts, rexp, gsel)


def _causal_steps(seq, tq, tk, descending):
    qi, kj, first, last = [], [], [], []
    for i in range(seq // tq):
        n = (i * tq + tq - 1) // tk + 1
        order = range(n - 1, -1, -1) if descending else range(n)
        for a, j in enumerate(order):
            qi.append(i), kj.append(j), first.append(int(a == 0)), last.append(int(a == n - 1))
    return tuple(jnp.asarray(np.asarray(v, np.int32)) for v in (qi, kj, first, last))


def _sb_weights(z, valid, ls, usel, latest_first):
    rows, nk = z.shape
    nb = nk // LANES
    sp = _softplus(z)
    l = -sp if valid is None else jnp.where(valid, -sp, 0.0)
    blocks = [l[:, b * LANES:(b + 1) * LANES] for b in range(nb)]
    later_in = _dot_sel_rhs(jnp.concatenate(blocks, axis=0), usel, 2)
    carry = [None] * nb
    for b in (range(nb) if latest_first else range(nb - 1, -1, -1)):
        carry[b] = ls
        ls = ls + jnp.sum(blocks[b], axis=-1, keepdims=True)
    later = jnp.concatenate([later_in[b * rows:(b + 1) * rows] + carry[b] for b in range(nb)], axis=1)
    w = jnp.exp(z - sp + later)
    if valid is not None:
        w = jnp.where(valid, w, 0.0)
    return w.astype(BF16), ls


def _suffix_sel():
    a = lax.broadcasted_iota(jnp.int32, (LANES, LANES), 0)
    b = lax.broadcasted_iota(jnp.int32, (LANES, LANES), 1)
    return (a > b).astype(BF16)


def _sb_prompt_kernel(qi_ref, kj_ref, first_ref, last_ref, q_ref, k_ref, v_ref, o_ref,
                      qs_ref, ls_ref, acc_ref, *, tq):
    s = pl.program_id(2)
    rows = SB_GROUP * tq

    @pl.when(first_ref[s] == 1)
    def _():
        qs_ref[...] = jnp.concatenate(
            [q_ref[:, g * SB_HEAD_DIM:(g + 1) * SB_HEAD_DIM] for g in range(SB_GROUP)], axis=0).astype(BF16)
        ls_ref[...] = jnp.zeros_like(ls_ref)
        acc_ref[...] = jnp.zeros_like(acc_ref)

    kb = k_ref[...].astype(BF16)
    vb = v_ref[...].astype(BF16)
    usel = _suffix_sel()

    def step(valid):
        w, ls = _sb_weights(_dot_nt(qs_ref[...], kb) * SB_SCALE, valid, ls_ref[...], usel, latest_first=False)
        ls_ref[...] = ls
        acc_ref[...] += _dot(w, vb)

    on_diagonal = kj_ref[s] == qi_ref[s]

    @pl.when(on_diagonal)
    def _():
        step(lax.broadcasted_iota(jnp.int32, (rows, tq), 1) < _row_mod((rows, tq), tq))

    @pl.when(jnp.logical_not(on_diagonal))
    def _():
        step(None)

    @pl.when(last_ref[s] == 1)
    def _():
        for g in range(SB_GROUP):
            o_ref[:, g * SB_HEAD_DIM:(g + 1) * SB_HEAD_DIM] = acc_ref[g * tq:(g + 1) * tq]


def _sb_prompt(proj, batch, seq):
    tq = min(seq, 512)
    tk = tq
    tabs = _causal_steps(seq, tq, tk, descending=True)
    nsteps = tabs[0].shape[0]
    nq, nkb = seq // tq, seq // tk
    kcol = SB_WIDTH // SB_HEAD_DIM
    vcol = (SB_WIDTH + SB_KV_WIDTH) // SB_HEAD_DIM
    rows = SB_GROUP * tq
    grid_spec = pltpu.PrefetchScalarGridSpec(
        num_scalar_prefetch=4,
        grid=(batch, SB_KV_HEADS, nsteps),
        in_specs=[
            pl.BlockSpec((tq, SB_GROUP * SB_HEAD_DIM), lambda b, h, s, qi, kj, f, l: (b * nq + qi[s], h)),
            pl.BlockSpec((tk, SB_HEAD_DIM), lambda b, h, s, qi, kj, f, l: (b * nkb + kj[s], kcol + h)),
            pl.BlockSpec((tk, SB_HEAD_DIM), lambda b, h, s, qi, kj, f, l: (b * nkb + kj[s], vcol + h)),
        ],
        out_specs=pl.BlockSpec((tq, SB_GROUP * SB_HEAD_DIM), lambda b, h, s, qi, kj, f, l: (b * nq + qi[s], h)),
        scratch_shapes=[pltpu.VMEM((rows, SB_HEAD_DIM), BF16), pltpu.VMEM((rows, 1), F32),
                        pltpu.VMEM((rows, SB_HEAD_DIM), F32)],
    )
    return pl.pallas_call(
        functools.partial(_sb_prompt_kernel, tq=tq),
        grid_spec=grid_spec,
        out_shape=jax.ShapeDtypeStruct((batch * seq, SB_WIDTH), F32),
        compiler_params=_cparams(("parallel", "parallel", "arbitrary")),
        name="sb_prompt",
    )(*tabs, proj, proj, proj)


def _sb_sample_kernel(pt_ref, q_ref, kn_ref, vn_ref, *refs, t, npp):
    k_refs, v_refs = refs[:npp], refs[npp:2 * npp]
    o_ref, ls_ref, acc_ref = refs[2 * npp:]
    c = pl.program_id(1)
    rows = SB_HEADS * t
    usel = _suffix_sel()
    qs = jnp.concatenate([q_ref[:, h * SB_HEAD_DIM:(h + 1) * SB_HEAD_DIM] for h in range(SB_HEADS)],
                         axis=0).astype(BF16)
    grows = SB_GROUP * t

    def attend(keys, values, valid, ls, acc):
        z = jnp.concatenate([_dot_nt(qs[h * grows:(h + 1) * grows], keys[h])
                             for h in range(SB_KV_HEADS)], axis=0) * SB_SCALE
        w, ls = _sb_weights(z, valid, ls, usel, latest_first=True)
        pv = jnp.concatenate([_dot(w[h * grows:(h + 1) * grows], values[h])
                              for h in range(SB_KV_HEADS)], axis=0)
        return ls, acc + pv

    @pl.when(c == 0)
    def _():
        pad = jnp.zeros((LANES - t, SB_HEAD_DIM), F32)

        def new_tokens(ref):
            return [jnp.concatenate([ref[:, h * SB_HEAD_DIM:(h + 1) * SB_HEAD_DIM], pad], axis=0).astype(BF16)
                    for h in range(SB_KV_HEADS)]

        tq = _row_mod((rows, LANES), t)
        kidx = lax.broadcasted_iota(jnp.int32, (rows, LANES), 1)
        ls, acc = attend(new_tokens(kn_ref), new_tokens(vn_ref), kidx < tq,
                         jnp.zeros((rows, 1), F32), jnp.zeros((rows, SB_HEAD_DIM), F32))
        ls_ref[...] = ls
        acc_ref[...] = acc

    def pages(page_refs):
        return [jnp.concatenate([r[0, pl.ds(h, LANES, stride=SB_KV_HEADS), :] for r in page_refs],
                                axis=0).astype(BF16) for h in range(SB_KV_HEADS)]

    ls, acc = attend(pages(k_refs), pages(v_refs), None, ls_ref[...], acc_ref[...])
    ls_ref[...] = ls
    acc_ref[...] = acc

    @pl.when(c == pl.num_programs(1) - 1)
    def _():
        for h in range(SB_HEADS):
            o_ref[:, h * SB_HEAD_DIM:(h + 1) * SB_HEAD_DIM] = acc[h * t:(h + 1) * t]


def _pages_per_step(n_pages):
    return max(p for p in range(1, 17) if n_pages % p == 0)


def _sb_sample(proj, cache_k, cache_v, j, page_table, batch, t):
    n_pool, page = cache_k.shape[1], cache_k.shape[2]
    assert page == LANES
    n_pages = page_table.shape[1]
    npp = _pages_per_step(n_pages)
    ck = cache_k.reshape(-1, page * SB_KV_HEADS, SB_HEAD_DIM)
    cv = cache_v.reshape(-1, page * SB_KV_HEADS, SB_HEAD_DIM)
    pt = (page_table + j * n_pool).reshape(-1)
    kcol = SB_WIDTH // SB_KV_WIDTH

    def page_spec(p):
        return pl.BlockSpec((1, page * SB_KV_HEADS, SB_HEAD_DIM),
                            lambda b, c, pt: (pt[b * n_pages + n_pages - 1 - (c * npp + p)], 0, 0))

    grid_spec = pltpu.PrefetchScalarGridSpec(
        num_scalar_prefetch=1,
        grid=(batch, n_pages // npp),
        in_specs=[
            pl.BlockSpec((t, SB_WIDTH), lambda b, c, pt: (b, 0)),
            pl.BlockSpec((t, SB_KV_WIDTH), lambda b, c, pt: (b, kcol)),
            pl.BlockSpec((t, SB_KV_WIDTH), lambda b, c, pt: (b, kcol + 1)),
        ] + [page_spec(p) for p in range(npp)] * 2,
        out_specs=pl.BlockSpec((t, SB_WIDTH), lambda b, c, pt: (b, 0)),
        scratch_shapes=[pltpu.VMEM((SB_HEADS * t, 1), F32), pltpu.VMEM((SB_HEADS * t, SB_HEAD_DIM), F32)],
    )
    return pl.pallas_call(
        functools.partial(_sb_sample_kernel, t=t, npp=npp),
        grid_spec=grid_spec,
        out_shape=jax.ShapeDtypeStruct((batch * t, SB_WIDTH), F32),
        compiler_params=_cparams(("parallel", "arbitrary")),
        name="sb_sample",
    )(pt, proj, proj, proj, *([ck] * npp), *([cv] * npp))


def _rope_tables(pos):
    half = MLA_ROPE // 2
    inv_freq = ROPE_THETA ** (-jnp.arange(half, dtype=F32) / half)
    ang = pos.astype(F32)[:, None] * inv_freq[None, :]
    cos, sin = jnp.cos(ang), jnp.sin(ang)
    zero = jnp.zeros_like(sin)
    reps = LANES // MLA_ROPE
    return (jnp.tile(jnp.concatenate([cos, cos], axis=1), (1, reps)),
            jnp.tile(jnp.concatenate([-sin, zero], axis=1), (1, reps)),
            jnp.tile(jnp.concatenate([zero, sin], axis=1), (1, reps)))


def _rope128(x, c, s_lo, s_hi):
    half = MLA_ROPE // 2
    return x * c + pltpu.roll(x, LANES - half, 1) * s_lo + pltpu.roll(x, half, 1) * s_hi


def _mla_prep_kernel(q_ref, ckv_ref, kr_ref, kvn_ref, wuk_ref, c_ref, slo_ref, shi_ref,
                     qlat_ref, qrope_ref, ckvn_ref, krr_ref):
    c, slo, shi = c_ref[...], slo_ref[...], shi_ref[...]
    for h in range(MLA_HEADS):
        qn = q_ref[:, h * MLA_NOPE:(h + 1) * MLA_NOPE].astype(BF16)
        qlat_ref[:, h * MLA_KV_LORA:(h + 1) * MLA_KV_LORA] = _dot(qn, wuk_ref[h])
    rope0 = MLA_HEADS * MLA_NOPE
    for i in range(MLA_HEADS * MLA_ROPE // LANES):
        qrope_ref[:, i * LANES:(i + 1) * LANES] = _rope128(
            q_ref[:, rope0 + i * LANES:rope0 + (i + 1) * LANES], c, slo, shi)
    ckv = ckv_ref[...]
    ms = jnp.mean(ckv * ckv, axis=-1, keepdims=True)
    ckvn_ref[...] = ckv * lax.rsqrt(ms + NORM_EPS) * kvn_ref[...]
    krr_ref[...] = _rope128(kr_ref[:, :LANES], c, slo, shi)[:, :MLA_ROPE]


def _mla_prep(q, proj, kv_norm, wuk_t, tabs):
    m = q.shape[0]
    tm = min(m, 512)
    qw = MLA_HEADS * (MLA_NOPE + MLA_ROPE)
    row = lambda w: pl.BlockSpec((tm, w), lambda i: (i, 0))
    return pl.pallas_call(
        _mla_prep_kernel,
        grid=(m // tm,),
        in_specs=[
            row(qw),
            pl.BlockSpec((tm, MLA_KV_LORA), lambda i: (i, MLA_Q_LORA // MLA_KV_LORA)),
            pl.BlockSpec((tm, MLA_KV_LORA), lambda i: (i, MLA_Q_LORA // MLA_KV_LORA + 1)),
            pl.BlockSpec((1, MLA_KV_LORA), lambda i: (0, 0)),
            pl.BlockSpec((MLA_HEADS, MLA_NOPE, MLA_KV_LORA), lambda i: (0, 0, 0)),
            row(LANES), row(LANES), row(LANES),
        ],
        out_specs=[row(MLA_HEADS * MLA_KV_LORA), row(MLA_HEADS * MLA_ROPE), row(MLA_KV_LORA), row(MLA_ROPE)],
        out_shape=[
            jax.ShapeDtypeStruct((m, MLA_HEADS * MLA_KV_LORA), F32),
            jax.ShapeDtypeStruct((m, MLA_HEADS * MLA_ROPE), F32),
            jax.ShapeDtypeStruct((m, MLA_KV_LORA), F32),
            jax.ShapeDtypeStruct((m, MLA_ROPE), F32),
        ],
        compiler_params=_cparams(("parallel",)),
        name="mla_prep",
    )(q, proj, proj, kv_norm.reshape(1, MLA_KV_LORA), wuk_t, *tabs)


MLA_QK = MLA_KV_LORA + LANES


def _mla_stack_q(qlat_ref, qrope_ref, t):
    pad = jnp.zeros((t, LANES - MLA_ROPE), F32)
    return jnp.concatenate(
        [jnp.concatenate([qlat_ref[:, h * MLA_KV_LORA:(h + 1) * MLA_KV_LORA],
                          qrope_ref[:, h * MLA_ROPE:(h + 1) * MLA_ROPE], pad], axis=1)
         for h in range(MLA_HEADS)], axis=0).astype(BF16)


def _mla_keys(ckv, kr, pad_rows=0):
    k = jnp.concatenate([ckv, kr, jnp.zeros((ckv.shape[0], LANES - MLA_ROPE), F32)], axis=1)
    if pad_rows:
        k = jnp.concatenate([k, jnp.zeros((pad_rows, MLA_QK), F32)], axis=0)
    return k.astype(BF16)


def _softmax_update(s, values, valid, m, l, acc):
    s = s * MLA_SCALE
    if valid is not None:
        s = jnp.where(valid, s, NEG_BIG)
    m_new = jnp.maximum(m, jnp.max(s, axis=-1, keepdims=True))
    corr = jnp.exp(m - m_new)
    p = jnp.exp(s - m_new)
    return (m_new, l * corr + jnp.sum(p, axis=-1, keepdims=True), acc * corr + _dot(p.astype(BF16), values))


def _mla_update(q, kcat, valid, m, l, acc):
    return _softmax_update(_dot_nt(q, kcat), kcat[:, :MLA_KV_LORA], valid, m, l, acc)


def _mla_finish(acc, l, wuv_ref, o_ref, t):
    o_lat = (acc / l).astype(BF16)
    for h in range(MLA_HEADS):
        o_ref[:, h * MLA_V:(h + 1) * MLA_V] = _dot(o_lat[h * t:(h + 1) * t], wuv_ref[h])


def _mla_prompt_kernel(qi_ref, kj_ref, first_ref, last_ref, qlat_ref, qrope_ref, ckv_ref, kr_ref, wuv_ref,
                       o_ref, q_ref, m_ref, l_ref, acc_ref, *, tq, tk):
    s = pl.program_id(1)
    rows = MLA_HEADS * tq

    @pl.when(first_ref[s] == 1)
    def _():
        q_ref[...] = _mla_stack_q(qlat_ref, qrope_ref, tq)
        m_ref[...] = jnp.full_like(m_ref, NEG_BIG)
        l_ref[...] = jnp.zeros_like(l_ref)
        acc_ref[...] = jnp.zeros_like(acc_ref)

    kcat = _mla_keys(ckv_ref[...], kr_ref[...])

    def step(valid):
        m, l, acc = _mla_update(q_ref[...], kcat, valid, m_ref[...], l_ref[...], acc_ref[...])
        m_ref[...] = m
        l_ref[...] = l
        acc_ref[...] = acc

    crosses = (kj_ref[s] + 1) * tk - 1 > qi_ref[s] * tq

    @pl.when(crosses)
    def _():
        qpos = qi_ref[s] * tq + _row_mod((rows, tk), tq)
        kpos = kj_ref[s] * tk + lax.broadcasted_iota(jnp.int32, (rows, tk), 1)
        step(kpos <= qpos)

    @pl.when(jnp.logical_not(crosses))
    def _():
        step(None)

    @pl.when(last_ref[s] == 1)
    def _():
        _mla_finish(acc_ref[...], l_ref[...], wuv_ref, o_ref, tq)


def _mla_prompt(qlat, qrope, ckvn, krr, wuv_t, batch, seq):
    tq = min(seq, 64)
    tk = min(seq, 512)
    tabs = _causal_steps(seq, tq, tk, descending=False)
    nsteps = tabs[0].shape[0]
    nq, nkb = seq // tq, seq // tk
    rows = MLA_HEADS * tq
    grid_spec = pltpu.PrefetchScalarGridSpec(
        num_scalar_prefetch=4,
        grid=(batch, nsteps),
        in_specs=[
            pl.BlockSpec((tq, MLA_HEADS * MLA_KV_LORA), lambda b, s, qi, kj, f, l: (b * nq + qi[s], 0)),
            pl.BlockSpec((tq, MLA_HEADS * MLA_ROPE), lambda b, s, qi, kj, f, l: (b * nq + qi[s], 0)),
            pl.BlockSpec((tk, MLA_KV_LORA), lambda b, s, qi, kj, f, l: (b * nkb + kj[s], 0)),
            pl.BlockSpec((tk, MLA_ROPE), lambda b, s, qi, kj, f, l: (b * nkb + kj[s], 0)),
            pl.BlockSpec((MLA_HEADS, MLA_KV_LORA, MLA_V), lambda b, s, qi, kj, f, l: (0, 0, 0)),
        ],
        out_specs=pl.BlockSpec((tq, MLA_WIDTH), lambda b, s, qi, kj, f, l: (b * nq + qi[s], 0)),
        scratch_shapes=[pltpu.VMEM((rows, MLA_QK), BF16),
                        pltpu.VMEM((rows, 1), F32), pltpu.VMEM((rows, 1), F32),
                        pltpu.VMEM((rows, MLA_KV_LORA), F32)],
    )
    return pl.pallas_call(
        functools.partial(_mla_prompt_kernel, tq=tq, tk=tk),
        grid_spec=grid_spec,
        out_shape=jax.ShapeDtypeStruct((batch * seq, MLA_WIDTH), F32),
        compiler_params=_cparams(("parallel", "arbitrary")),
        name="mla_prompt",
    )(*tabs, qlat, qrope, ckvn, krr, wuv_t)


def _mla_sample_kernel(pt_ref, qlat_ref, qrope_ref, cn_ref, kn_ref, wuv_ref, *refs, t, npp):
    c_refs, k_refs = refs[:npp], refs[npp:2 * npp]
    o_ref, m_ref, l_ref, acc_ref = refs[2 * npp:]
    c = pl.program_id(1)
    rows = MLA_HEADS * t
    q = _mla_stack_q(qlat_ref, qrope_ref, t)

    @pl.when(c == 0)
    def _():
        tq = _row_mod((rows, LANES), t)
        kidx = lax.broadcasted_iota(jnp.int32, (rows, LANES), 1)
        m, l, acc = _mla_update(q, _mla_keys(cn_ref[...], kn_ref[...], LANES - t), kidx <= tq,
                                jnp.full((rows, 1), NEG_BIG, F32), jnp.zeros((rows, 1), F32),
                                jnp.zeros((rows, MLA_KV_LORA), F32))
        m_ref[...] = m
        l_ref[...] = l
        acc_ref[...] = acc

    ckv = jnp.concatenate([r[0] for r in c_refs], axis=0).astype(BF16)
    kr_t = jnp.concatenate([r[0] for r in k_refs], axis=1)
    kr_t = jnp.concatenate([kr_t, jnp.zeros((LANES - MLA_ROPE, kr_t.shape[1]), F32)], axis=0).astype(BF16)
    s = _dot_nt(q[:, :MLA_KV_LORA], ckv) + _dot(q[:, MLA_KV_LORA:], kr_t)
    m, l, acc = _softmax_update(s, ckv, None, m_ref[...], l_ref[...], acc_ref[...])
    m_ref[...] = m
    l_ref[...] = l
    acc_ref[...] = acc

    @pl.when(c == pl.num_programs(1) - 1)
    def _():
        _mla_finish(acc, l, wuv_ref, o_ref, t)


def _mla_sample(qlat, qrope, ckvn, krr, wuv_t, cache_ckv, cache_kr, j, page_table, batch, t):
    n_pool, page = cache_ckv.shape[1], cache_ckv.shape[2]
    assert page == LANES
    n_pages = page_table.shape[1]
    npp = _pages_per_step(n_pages)
    cc = cache_ckv.reshape(-1, page, MLA_KV_LORA)
    ck = jnp.swapaxes(cache_kr, 2, 3).reshape(-1, MLA_ROPE, page)
    pt = (page_table + j * n_pool).reshape(-1)
    rows = MLA_HEADS * t

    def page_spec(p, shape):
        return pl.BlockSpec((1,) + shape, lambda b, c, pt: (pt[b * n_pages + c * npp + p], 0, 0))

    grid_spec = pltpu.PrefetchScalarGridSpec(
        num_scalar_prefetch=1,
        grid=(batch, n_pages // npp),
        in_specs=[
            pl.BlockSpec((t, MLA_HEADS * MLA_KV_LORA), lambda b, c, pt: (b, 0)),
            pl.BlockSpec((t, MLA_HEADS * MLA_ROPE), lambda b, c, pt: (b, 0)),
            pl.BlockSpec((t, MLA_KV_LORA), lambda b, c, pt: (b, 0)),
            pl.BlockSpec((t, MLA_ROPE), lambda b, c, pt: (b, 0)),
            pl.BlockSpec((MLA_HEADS, MLA_KV_LORA, MLA_V), lambda b, c, pt: (0, 0, 0)),
        ] + [page_spec(p, (page, MLA_KV_LORA)) for p in range(npp)]
          + [page_spec(p, (MLA_ROPE, page)) for p in range(npp)],
        out_specs=pl.BlockSpec((t, MLA_WIDTH), lambda b, c, pt: (b, 0)),
        scratch_shapes=[pltpu.VMEM((rows, 1), F32), pltpu.VMEM((rows, 1), F32),
                        pltpu.VMEM((rows, MLA_KV_LORA), F32)],
    )
    return pl.pallas_call(
        functools.partial(_mla_sample_kernel, t=t, npp=npp),
        grid_spec=grid_spec,
        out_shape=jax.ShapeDtypeStruct((batch * t, MLA_WIDTH), F32),
        compiler_params=_cparams(("parallel", "arbitrary")),
        name="mla_sample",
    )(pt, qlat, qrope, ckvn, krr, wuv_t, *([cc] * npp), *([ck] * npp))


def kernel(x_prompt, x_sample, state_ssd_conv, state_ssd_h, cache_sb_k, cache_sb_v, cache_mla_ckv,
           cache_mla_krope, page_table, norm_pre, norm_post, ssd_w_in, ssd_conv_w, ssd_conv_b, ssd_dt_bias,
           ssd_a_log, ssd_d, ssd_norm_w, ssd_w_out, sb_w_in, sb_w_out, mla_w_in, mla_q_norm, mla_kv_norm,
           mla_w_uq, mla_w_uk, mla_w_uv, mla_w_out):
    bp, seq, d = x_prompt.shape
    bs, t, _ = x_sample.shape
    depth = norm_pre.shape[0]
    past = page_table.shape[1] * cache_sb_k.shape[2]
    assert seq % SSD_CHUNK == 0 and t >= SSD_CONV - 1
    xp = x_prompt.reshape(bp * seq, d)
    xs = x_sample.reshape(bs * t, d)
    outs = {k: [] for k in ("p_conv", "p_h", "p_k", "p_v", "p_ckv", "p_kr",
                            "s_conv", "s_h", "s_k", "s_v", "s_ckv", "s_kr")}
    zx = SSD_D_INNER + SSD_CONV_DIM
    for i in range(depth):
        kind, j = i % N_MIXERS, i // N_MIXERS
        if kind == 0:
            w_zx = ssd_w_in[j][:, :zx].astype(BF16)
            w_dt = jnp.pad(ssd_w_in[j][:, zx:], ((0, 0), (0, LANES - SSD_HEADS))).astype(BF16)
            w_out = ssd_w_out[j].astype(BF16)
            consts = _ssd_consts(ssd_conv_w[j], ssd_conv_b[j], ssd_dt_bias[j], ssd_a_log[j], ssd_d[j],
                                 ssd_norm_w[j])
            proj_p = _norm_matmul(xp, norm_pre[i], w_zx, name="ssd_in_p")
            proj_s = _norm_matmul(xs, norm_pre[i], w_zx, name="ssd_in_s")
            dt_p = _norm_matmul(xp, norm_pre[i], w_dt, name="ssd_dt_p")
            dt_s = _norm_matmul(xs, norm_pre[i], w_dt, name="ssd_dt_s")
            y_p, h_p = _ssd_prompt(proj_p, dt_p, consts, bp, seq)
            h0 = state_ssd_h.reshape(-1, SSD_D_INNER, SSD_STATE)
            y_s, h_s = _ssd_sample(proj_s, dt_s, state_ssd_conv[j], h0, j, consts, bs, t)
            outs["p_conv"].append(proj_p.reshape(bp, seq, zx)[:, seq - (SSD_CONV - 1):, SSD_D_INNER:])
            outs["s_conv"].append(proj_s.reshape(bs, t, zx)[:, t - (SSD_CONV - 1):, SSD_D_INNER:])
            outs["p_h"].append(h_p.reshape(bp, SSD_HEADS, SSD_HEAD_DIM, SSD_STATE))
            outs["s_h"].append(h_s.reshape(bs, SSD_HEADS, SSD_HEAD_DIM, SSD_STATE))
            xp = _matmul_post(y_p, w_out, xp, norm_post[i], name="ssd_out_p")
            xs = _matmul_post(y_s, w_out, xs, norm_post[i], name="ssd_out_s")
        elif kind == 1:
            w_in = sb_w_in[j].astype(BF16)
            w_out = sb_w_out[j].astype(BF16)
            proj_p = _norm_matmul(xp, norm_pre[i], w_in, name="sb_in_p")
            proj_s = _norm_matmul(xs, norm_pre[i], w_in, name="sb_in_s")
            gcol = SB_WIDTH + 2 * SB_KV_WIDTH
            a_p = _sb_prompt(proj_p, bp, seq)
            a_s = _sb_sample(proj_s, cache_sb_k, cache_sb_v, j, page_table, bs, t)
            for tag, pr, nb, ln in (("p", proj_p, bp, seq), ("s", proj_s, bs, t)):
                kv = pr.reshape(nb, ln, -1)
                outs[tag + "_k"].append(kv[:, :, SB_WIDTH:SB_WIDTH + SB_KV_WIDTH].reshape(nb, ln, SB_KV_HEADS, SB_HEAD_DIM))
                outs[tag + "_v"].append(kv[:, :, SB_WIDTH + SB_KV_WIDTH:gcol].reshape(nb, ln, SB_KV_HEADS, SB_HEAD_DIM))
            xp = _matmul_post(a_p, w_out, xp, norm_post[i], gate=proj_p, gate_col=gcol, name="sb_out_p")
            xs = _matmul_post(a_s, w_out, xs, norm_post[i], gate=proj_s, gate_col=gcol, name="sb_out_s")
        else:
            o1, o2, o3 = MLA_Q_LORA, MLA_Q_LORA + MLA_KV_LORA, MLA_Q_LORA + MLA_KV_LORA + MLA_ROPE
            w = mla_w_in[j]
            w_in = jnp.concatenate([w[:, :o3], jnp.zeros((d, MLA_KV_LORA - MLA_ROPE), F32), w[:, o3:]],
                                   axis=1).astype(BF16)
            gcol = o2 + MLA_KV_LORA
            uq = mla_w_uq[j]
            w_uq = jnp.concatenate([uq[:, :, :MLA_NOPE].reshape(MLA_Q_LORA, -1),
                                    uq[:, :, MLA_NOPE:].reshape(MLA_Q_LORA, -1)], axis=1).astype(BF16)
            wuk_t = jnp.transpose(mla_w_uk[j], (1, 2, 0)).astype(BF16)
            wuv_t = jnp.transpose(mla_w_uv[j], (1, 0, 2)).astype(BF16)
            w_out = mla_w_out[j].astype(BF16)
            res = []
            for tag, xx, nb, ln, pos in (("p", xp, bp, seq, jnp.arange(seq)), ("s", xs, bs, t, past + jnp.arange(t))):
                proj = _norm_matmul(xx, norm_pre[i], w_in, name="mla_in_" + tag)
                q = _norm_matmul(proj, mla_q_norm[j], w_uq, name="mla_uq_" + tag)
                tabs = tuple(jnp.tile(tb, (nb, 1)) for tb in _rope_tables(pos))
                qlat, qrope, ckvn, krr = _mla_prep(q, proj, mla_kv_norm[j], wuk_t, tabs)
                outs[tag + "_ckv"].append(ckvn.reshape(nb, ln, MLA_KV_LORA))
                outs[tag + "_kr"].append(krr.reshape(nb, ln, MLA_ROPE))
                if tag == "p":
                    o = _mla_prompt(qlat, qrope, ckvn, krr, wuv_t, nb, ln)
                else:
                    o = _mla_sample(qlat, qrope, ckvn, krr, wuv_t, cache_mla_ckv, cache_mla_krope, j,
                                    page_table, nb, ln)
                res.append(_matmul_post(o, w_out, xx, norm_post[i], gate=proj, gate_col=gcol,
                                        name="mla_out_" + tag))
            xp, xs = res
    st = lambda k: jnp.stack(outs[k])
    return (xp.reshape(bp, seq, d), xs.reshape(bs, t, d),
            st("p_conv"), st("p_h"), st("p_k"), st("p_v"), st("p_ckv"), st("p_kr"),
            st("s_conv"), st("s_h"), st("s_k"), st("s_v"), st("s_ckv"), st("s_kr"))
```

```python
import functools
import math

import numpy as np
import jax
import jax.numpy as jnp
from jax import lax
from jax.experimental import pallas as pl
from jax.experimental.pallas import tpu as pltpu

F32 = jnp.float32
BF16 = jnp.bfloat16

D_MODEL = 2048
N_MIXERS = 3
NORM_EPS = 1e-6

SSD_D_INNER = 2 * D_MODEL
SSD_HEAD_DIM = 64
SSD_HEADS = SSD_D_INNER // SSD_HEAD_DIM
SSD_GROUPS = 8
SSD_STATE = 128
SSD_CONV = 4
SSD_CHUNK = 128
SSD_BC = SSD_GROUPS * SSD_STATE
SSD_CONV_DIM = SSD_D_INNER + 2 * SSD_BC
SSD_GROUP_WIDTH = SSD_D_INNER // SSD_GROUPS
SSD_HEADS_PER_GROUP = SSD_HEADS // SSD_GROUPS

SB_HEADS = 16
SB_KV_HEADS = 4
SB_GROUP = SB_HEADS // SB_KV_HEADS
SB_HEAD_DIM = 128
SB_WIDTH = SB_HEADS * SB_HEAD_DIM
SB_KV_WIDTH = SB_KV_HEADS * SB_HEAD_DIM
SB_SCALE = SB_HEAD_DIM ** -0.5

MLA_HEADS = 16
MLA_Q_LORA = 512
MLA_KV_LORA = 256
MLA_NOPE = 128
MLA_ROPE = 64
MLA_V = 128
MLA_WIDTH = MLA_HEADS * MLA_V
MLA_SCALE = (MLA_NOPE + MLA_ROPE) ** -0.5
ROPE_THETA = 10000.0

LANES = 128
SUBLANES = 8
VMEM_LIMIT_BYTES = 48 * 1024 * 1024
NEG_BIG = -1e30
LOG2E = math.log2(math.e)
SB_QSCALE = SB_SCALE * LOG2E
MLA_QSCALE = MLA_SCALE * LOG2E


def _cparams(sem):
    return pltpu.CompilerParams(dimension_semantics=sem, vmem_limit_bytes=VMEM_LIMIT_BYTES)


def _dot(a, b):
    return jnp.dot(a, b, preferred_element_type=F32)


def _dot_nt(a, b):
    return lax.dot_general(a, b, (((1,), (1,)), ((), ())), preferred_element_type=F32)


def _dot_tn(a, b):
    return lax.dot_general(a, b, (((0,), (0,)), ((), ())), preferred_element_type=F32)


def _split_bf16(a, terms):
    parts, r = [], a
    for i in range(terms):
        p = r.astype(BF16)
        parts.append(p)
        if i + 1 < terms:
            r = r - p.astype(F32)
    return parts


def _dot_sel_rhs(a, sel, terms):
    out = None
    for p in _split_bf16(a, terms):
        d = _dot(p, sel)
        out = d if out is None else out + d
    return out


def _dot_sel_lhs(sel, a, terms):
    out = None
    for p in _split_bf16(a, terms):
        d = _dot(sel, p)
        out = d if out is None else out + d
    return out


def _row_mod(shape, n):
    assert n & (n - 1) == 0
    return lax.broadcasted_iota(jnp.int32, shape, 0) & (n - 1)


def _softplus(x):
    return jnp.maximum(x, 0.0) + jnp.log1p(jnp.exp(-jnp.abs(x)))


def _softplus2(x):
    return jnp.maximum(x, 0.0) + jnp.log(1.0 + jnp.exp2(-jnp.abs(x))) * LOG2E


def _silu(x):
    return x * jax.nn.sigmoid(x)


def _norm_matmul_kernel(x_ref, nw_ref, w_ref, o_ref, xn_ref):
    @pl.when(pl.program_id(1) == 0)
    def _():
        x = x_ref[...]
        ms = jnp.mean(x * x, axis=-1, keepdims=True)
        xn_ref[...] = (x * lax.rsqrt(ms + NORM_EPS) * nw_ref[...]).astype(BF16)

    o_ref[...] = _dot(xn_ref[...], w_ref[...])


def _norm_matmul(x, nw, w, *, col_block=0, name):
    m = x.shape[0]
    k, n = w.shape
    tm = min(m, 1024)
    tn = min(n, 512)
    return pl.pallas_call(
        _norm_matmul_kernel,
        grid=(m // tm, n // tn),
        in_specs=[
            pl.BlockSpec((tm, k), lambda i, j: (i, col_block)),
            pl.BlockSpec((1, k), lambda i, j: (0, 0)),
            pl.BlockSpec((k, tn), lambda i, j: (0, j)),
        ],
        out_specs=pl.BlockSpec((tm, tn), lambda i, j: (i, j)),
        out_shape=jax.ShapeDtypeStruct((m, n), F32),
        scratch_shapes=[pltpu.VMEM((tm, k), BF16)],
        compiler_params=_cparams(("parallel", "arbitrary")),
        name=name,
    )(x, nw.reshape(1, k), w)


def _post_kernel(*refs, gated, nk):
    if gated:
        a_ref, g_ref, w_ref, res_ref, pw_ref, o_ref, acc_ref = refs
    else:
        a_ref, w_ref, res_ref, pw_ref, o_ref, acc_ref = refs
    kk = pl.program_id(1)
    a = a_ref[...].astype(F32)
    if gated:
        a = a * _silu(g_ref[...])
    part = _dot(a.astype(BF16), w_ref[...])

    @pl.when(kk == 0)
    def _():
        acc_ref[...] = part

    @pl.when(kk > 0)
    def _():
        acc_ref[...] += part

    @pl.when(kk == nk - 1)
    def _():
        y = acc_ref[...]
        ms = jnp.mean(y * y, axis=-1, keepdims=True)
        o_ref[...] = res_ref[...] + y * lax.rsqrt(ms + NORM_EPS) * pw_ref[...]


def _matmul_post(a, w, res, pw, *, gate=None, gate_col=0, name):
    m, k = a.shape
    n = w.shape[1]
    tm = min(m, 512)
    tk = min(k, 512)
    nk = k // tk
    gated = gate is not None
    in_specs = [pl.BlockSpec((tm, tk), lambda i, kk: (i, kk))]
    args = [a]
    if gated:
        gb = gate_col // tk
        in_specs.append(pl.BlockSpec((tm, tk), lambda i, kk: (i, gb + kk)))
        args.append(gate)
    in_specs += [
        pl.BlockSpec((tk, n), lambda i, kk: (kk, 0)),
        pl.BlockSpec((tm, n), lambda i, kk: (i, 0)),
        pl.BlockSpec((1, n), lambda i, kk: (0, 0)),
    ]
    args += [w, res, pw.reshape(1, n)]
    return pl.pallas_call(
        functools.partial(_post_kernel, gated=gated, nk=nk),
        grid=(m // tm, nk),
        in_specs=in_specs,
        out_specs=pl.BlockSpec((tm, n), lambda i, kk: (i, 0)),
        out_shape=jax.ShapeDtypeStruct((m, n), F32),
        scratch_shapes=[pltpu.VMEM((tm, n), F32)],
        compiler_params=_cparams(("parallel", "arbitrary")),
        name=name,
    )(*args)


def _conv_taps(ext, cur, w, bias, rows):
    off = SUBLANES - (SSD_CONV - 1)
    acc = ext[off:off + rows] * w[0:1]
    for k in range(1, SSD_CONV - 1):
        acc = acc + ext[off + k:off + k + rows] * w[k:k + 1]
    acc = acc + cur * w[SSD_CONV - 1:SSD_CONV]
    return _silu(acc + bias)


def _gate_norm(y, z, nw):
    y = y * _silu(z)
    ms = jnp.mean(y * y, axis=-1, keepdims=True)
    return y * lax.rsqrt(ms + NORM_EPS) * nw


def _ssd_prompt_kernel(z_ref, xs_ref, bm_ref, cm_ref, dt_ref, cw_ref, cb_ref, dtb_ref, alog_ref,
                       dsk_ref, nw_ref, y_ref, h_ref, tail_ref):
    cs = SSD_CHUNK

    @pl.when(pl.program_id(1) == 0)
    def _():
        h_ref[...] = jnp.zeros_like(h_ref)
        tail_ref[...] = jnp.zeros_like(tail_ref)

    def conv(cur, lo, width):
        ext = jnp.concatenate([tail_ref[:, lo:lo + width], cur], axis=0)
        out = _conv_taps(ext, cur, cw_ref[:, lo:lo + width], cb_ref[:, lo:lo + width], cs)
        tail_ref[:, lo:lo + width] = cur[cs - SUBLANES:cs]
        return out

    row = lax.broadcasted_iota(jnp.int32, (cs, cs), 0)
    col = lax.broadcasted_iota(jnp.int32, (cs, cs), 1)
    causal = row >= col
    tril = causal.astype(BF16)
    left = col < SSD_HEAD_DIM

    dt = _softplus(dt_ref[...] + dtb_ref[...])
    la = dt * (-jnp.exp(alog_ref[...]))
    lc = _dot_sel_lhs(tril, la, 3)
    lc_t = lc.T
    dt_t = dt.T
    e_all = jnp.exp(lc)
    lc_last = lc[cs - 1:cs, :]
    de_all = jnp.exp(lc_last - lc) * dt
    cd_all = jnp.exp(lc_last)

    b_act = conv(bm_ref[...], SSD_D_INNER, SSD_BC)
    c_act = conv(cm_ref[...], SSD_D_INNER + SSD_BC, SSD_BC)

    for g in range(SSD_GROUPS):
        gl = g * SSD_GROUP_WIDTH
        xg = conv(xs_ref[:, gl:gl + SSD_GROUP_WIDTH], gl, SSD_GROUP_WIDTH)
        bg = b_act[:, g * SSD_STATE:(g + 1) * SSD_STATE].astype(BF16)
        cg = c_act[:, g * SSD_STATE:(g + 1) * SSD_STATE].astype(BF16)
        cbm = _dot_nt(cg, bg)
        h_prev = h_ref[0, gl:gl + SSD_GROUP_WIDTH, :]
        y_off = _dot_nt(cg, h_prev.astype(BF16))
        y_parts, xw_parts, cd_parts = [], [], []
        for pr in range(SSD_HEADS_PER_GROUP // 2):
            xp = xg[:, pr * LANES:(pr + 1) * LANES]
            ws, es, des = [], [], []
            for hh in (0, 1):
                h = g * SSD_HEADS_PER_GROUP + 2 * pr + hh
                seg = lc[:, h:h + 1] - lc_t[h:h + 1, :]
                decay = jnp.where(causal, jnp.exp(jnp.where(causal, seg, 0.0)), 0.0)
                ws.append(cbm * decay * dt_t[h:h + 1, :])
                es.append(jnp.broadcast_to(e_all[:, h:h + 1], (cs, LANES)))
                des.append(jnp.broadcast_to(de_all[:, h:h + 1], (cs, LANES)))
                cd_parts.append(jnp.broadcast_to(cd_all[:, h:h + 1], (SSD_HEAD_DIM, SSD_STATE)))
            w2 = jnp.concatenate(ws, axis=1).astype(BF16)
            x2 = jnp.concatenate([jnp.where(left, xp, 0.0), jnp.where(left, 0.0, xp)],
                                 axis=0).astype(BF16)
            y_diag = _dot(w2, x2)
            e2 = jnp.where(left, es[0], es[1])
            y_parts.append(y_diag + y_off[:, pr * LANES:(pr + 1) * LANES] * e2
                           + dsk_ref[:, gl + pr * LANES:gl + (pr + 1) * LANES] * xp)
            xw_parts.append(xp * jnp.where(left, des[0], des[1]))
        yg = jnp.concatenate(y_parts, axis=1)
        y_ref[:, gl:gl + SSD_GROUP_WIDTH] = _gate_norm(
            yg, z_ref[:, gl:gl + SSD_GROUP_WIDTH], nw_ref[:, gl:gl + SSD_GROUP_WIDTH]).astype(y_ref.dtype)
        xw = jnp.concatenate(xw_parts, axis=1).astype(BF16)
        states = _dot_tn(xw, bg)
        h_ref[0, gl:gl + SSD_GROUP_WIDTH, :] = h_prev * jnp.concatenate(cd_parts, axis=0) + states


def _ssd_consts(conv_w, conv_b, dt_bias, a_log, d_skip, norm_w):
    pad = LANES - SSD_HEADS
    return (conv_w, conv_b.reshape(1, SSD_CONV_DIM),
            jnp.pad(dt_bias, (0, pad)).reshape(1, LANES), jnp.pad(a_log, (0, pad)).reshape(1, LANES),
            jnp.repeat(d_skip, SSD_HEAD_DIM).reshape(1, SSD_D_INNER), norm_w.reshape(1, SSD_D_INNER))


def _ssd_prompt(proj, dtraw, consts, batch, seq):
    cs = SSD_CHUNK
    nc = seq // cs
    zb = SSD_D_INNER // SSD_BC
    const_specs = [
        pl.BlockSpec((SSD_CONV, SSD_CONV_DIM), lambda b, c: (0, 0)),
        pl.BlockSpec((1, SSD_CONV_DIM), lambda b, c: (0, 0)),
        pl.BlockSpec((1, LANES), lambda b, c: (0, 0)),
        pl.BlockSpec((1, LANES), lambda b, c: (0, 0)),
        pl.BlockSpec((1, SSD_D_INNER), lambda b, c: (0, 0)),
        pl.BlockSpec((1, SSD_D_INNER), lambda b, c: (0, 0)),
    ]
    return pl.pallas_call(
        _ssd_prompt_kernel,
        grid=(batch, nc),
        in_specs=[
            pl.BlockSpec((cs, SSD_D_INNER), lambda b, c: (b * nc + c, 0)),
            pl.BlockSpec((cs, SSD_D_INNER), lambda b, c: (b * nc + c, 1)),
            pl.BlockSpec((cs, SSD_BC), lambda b, c: (b * nc + c, 2 * zb)),
            pl.BlockSpec((cs, SSD_BC), lambda b, c: (b * nc + c, 2 * zb + 1)),
            pl.BlockSpec((cs, LANES), lambda b, c: (b * nc + c, 0)),
        ] + const_specs,
        out_specs=[
            pl.BlockSpec((cs, SSD_D_INNER), lambda b, c: (b * nc + c, 0)),
            pl.BlockSpec((1, SSD_D_INNER, SSD_STATE), lambda b, c: (b, 0, 0)),
        ],
        out_shape=[
            jax.ShapeDtypeStruct((batch * seq, SSD_D_INNER), BF16),
            jax.ShapeDtypeStruct((batch, SSD_D_INNER, SSD_STATE), F32),
        ],
        scratch_shapes=[pltpu.VMEM((SUBLANES, SSD_CONV_DIM), F32)],
        compiler_params=_cparams(("parallel", "arbitrary")),
        name="ssd_prompt",
    )(proj, proj, proj, proj, dtraw, *consts)


def _ssd_sample_kernel(z_ref, xs_ref, bm_ref, cm_ref, dt_ref, st_ref, h0_ref, cw_ref, cb_ref, dtb_ref,
                       alog_ref, dsk_ref, nw_ref, rexp_ref, gsel_ref, y_ref, h_ref):
    t = xs_ref.shape[0]
    pairs = t * t

    def conv(cur, lo, width):
        ext = jnp.concatenate([st_ref[0, :, lo:lo + width], cur], axis=0)
        return _conv_taps(ext, cur, cw_ref[:, lo:lo + width], cb_ref[:, lo:lo + width], t)

    def tile_t(a):
        return jnp.concatenate([a] * t, axis=0)

    def tile_s(a):
        return jnp.concatenate([jnp.broadcast_to(a[s:s + 1], a.shape) for s in range(t)], axis=0)

    x_act = conv(xs_ref[...], 0, SSD_D_INNER)
    b_act = conv(bm_ref[...], SSD_D_INNER, SSD_BC)
    c_act = conv(cm_ref[...], SSD_D_INNER + SSD_BC, SSD_BC)

    dt = _softplus(dt_ref[...] + dtb_ref[...])
    la = dt * (-jnp.exp(alog_ref[...]))
    rix = lax.broadcasted_iota(jnp.int32, (t, LANES), 0)
    lc = la
    sh = 1
    while sh < t:
        lc = lc + jnp.where(rix >= sh, pltpu.roll(lc, sh, 0), 0.0)
        sh *= 2
    lc_last = lc[t - 1:t, :]
    rexp = rexp_ref[...]

    r = lax.broadcasted_iota(jnp.int32, (pairs, LANES), 0)
    ok = lax.shift_right_logical(r, int(math.log2(t))) <= (r & (t - 1))
    decay = jnp.where(ok, jnp.exp(jnp.where(ok, tile_t(lc) - tile_s(lc), 0.0)), 0.0)
    cbe = _dot_sel_rhs(tile_t(c_act) * tile_s(b_act), gsel_ref[...], 3)
    coef = _dot_sel_rhs(cbe * decay * tile_s(dt), rexp, 3)
    y = dsk_ref[...] * x_act
    for s in range(t):
        y = y + coef[s * t:(s + 1) * t] * jnp.broadcast_to(x_act[s:s + 1], x_act.shape)

    e_exp = _dot_sel_rhs(jnp.exp(lc), rexp, 3)
    xw = x_act * _dot_sel_rhs(jnp.exp(lc_last - lc) * dt, rexp, 3)
    cd_all = jnp.exp(lc_last)
    zero_rows = jnp.zeros((LANES - t, SSD_GROUP_WIDTH), F32)
    zero_b = jnp.zeros((LANES - t, SSD_STATE), F32)
    for g in range(SSD_GROUPS):
        gl = g * SSD_GROUP_WIDTH
        bg = b_act[:, g * SSD_STATE:(g + 1) * SSD_STATE]
        cg = c_act[:, g * SSD_STATE:(g + 1) * SSD_STATE]
        h_prev = h0_ref[0, gl:gl + SSD_GROUP_WIDTH, :]
        cg16 = jnp.concatenate([cg, jnp.zeros_like(cg)], axis=0).astype(BF16)
        y_off = _dot_nt(cg16, h_prev.astype(BF16))[:t]
        yg = y[:, gl:gl + SSD_GROUP_WIDTH] + y_off * e_exp[:, gl:gl + SSD_GROUP_WIDTH]
        y_ref[:, gl:gl + SSD_GROUP_WIDTH] = _gate_norm(
            yg, z_ref[:, gl:gl + SSD_GROUP_WIDTH], nw_ref[:, gl:gl + SSD_GROUP_WIDTH]).astype(y_ref.dtype)
        xw_pad = jnp.concatenate([xw[:, gl:gl + SSD_GROUP_WIDTH], zero_rows], axis=0).astype(BF16)
        b_pad = jnp.concatenate([bg, zero_b], axis=0).astype(BF16)
        states = _dot_tn(xw_pad, b_pad)
        cd = jnp.concatenate(
            [jnp.broadcast_to(cd_all[:, h:h + 1], (SSD_HEAD_DIM, SSD_STATE))
             for h in range(g * SSD_HEADS_PER_GROUP, (g + 1) * SSD_HEADS_PER_GROUP)], axis=0)
        h_ref[0, gl:gl + SSD_GROUP_WIDTH, :] = h_prev * cd + states


def _ssd_sample(proj, dtraw, conv_state, h0, j, consts, batch, t):
    assert t == SUBLANES, "sample SSD path handles one sublane tile of new tokens"
    zb = SSD_D_INNER // SSD_BC
    hist = jnp.pad(conv_state, ((0, 0), (SUBLANES - (SSD_CONV - 1), 0), (0, 0)))
    heads = np.arange(LANES)
    rexp = (heads[:, None] == (np.arange(SSD_D_INNER) // SSD_HEAD_DIM)[None, :])
    gsel = ((np.arange(SSD_BC) // SSD_STATE)[:, None] == (heads // SSD_HEADS_PER_GROUP)[None, :])
    gsel = gsel & (heads < SSD_HEADS)[None, :]
    rexp = jnp.asarray(rexp, BF16)
    gsel = jnp.asarray(gsel, BF16)
    full = lambda shape: pl.BlockSpec(shape, lambda b: (0,) * len(shape))
    return pl.pallas_call(
        _ssd_sample_kernel,
        grid=(batch,),
        in_specs=[
            pl.BlockSpec((t, SSD_D_INNER), lambda b: (b, 0)),
            pl.BlockSpec((t, SSD_D_INNER), lambda b: (b, 1)),
            pl.BlockSpec((t, SSD_BC), lambda b: (b, 2 * zb)),
            pl.BlockSpec((t, SSD_BC), lambda b: (b, 2 * zb + 1)),
            pl.BlockSpec((t, LANES), lambda b: (b, 0)),
            pl.BlockSpec((1, SUBLANES, SSD_CONV_DIM), lambda b: (b, 0, 0)),
            pl.BlockSpec((1, SSD_D_INNER, SSD_STATE), lambda b: (j * batch + b, 0, 0)),
            full((SSD_CONV, SSD_CONV_DIM)), full((1, SSD_CONV_DIM)), full((1, LANES)), full((1, LANES)),
            full((1, SSD_D_INNER)), full((1, SSD_D_INNER)),
            full((LANES, SSD_D_INNER)), full((SSD_BC, LANES)),
        ],
        out_specs=[
            pl.BlockSpec((t, SSD_D_INNER), lambda b: (b, 0)),
            pl.BlockSpec((1, SSD_D_INNER, SSD_STATE), lambda b: (b, 0, 0)),
        ],
        out_shape=[
            jax.ShapeDtypeStruct((batch * t, SSD_D_INNER), F32),
            jax.ShapeDtypeStruct((batch, SSD_D_INNER, SSD_STATE), F32),
        ],
        compiler_params=_cparams(("parallel",)),
        name="ssd_sample",
    )(proj, proj, proj, proj, dtraw, hist, h0, *consts, rexp, gsel)


def _causal_steps(seq, tq, tk, descending):
    qi, kj, first, last = [], [], [], []
    for i in range(seq // tq):
        n = (i * tq + tq - 1) // tk + 1
        order = range(n - 1, -1, -1) if descending else range(n)
        for a, j in enumerate(order):
            qi.append(i), kj.append(j), first.append(int(a == 0)), last.append(int(a == n - 1))
    return tuple(jnp.asarray(np.asarray(v, np.int32)) for v in (qi, kj, first, last))


def _sb_weights(z, valid, ls, usel, latest_first):
    rows, nk = z.shape
    nb = nk // LANES
    sp = _softplus2(z)
    l = -sp if valid is None else jnp.where(valid, -sp, 0.0)
    blocks = [l[:, b * LANES:(b + 1) * LANES] for b in range(nb)]
    later_in = _dot_sel_rhs(jnp.concatenate(blocks, axis=0), usel, 2)
    carry = [None] * nb
    for b in (range(nb) if latest_first else range(nb - 1, -1, -1)):
        carry[b] = ls
        ls = ls + jnp.sum(blocks[b], axis=-1, keepdims=True)
    later = jnp.concatenate([later_in[b * rows:(b + 1) * rows] + carry[b] for b in range(nb)], axis=1)
    w = jnp.exp2(z - sp + later)
    if valid is not None:
        w = jnp.where(valid, w, 0.0)
    return w.astype(BF16), ls


def _suffix_sel():
    a = lax.broadcasted_iota(jnp.int32, (LANES, LANES), 0)
    b = lax.broadcasted_iota(jnp.int32, (LANES, LANES), 1)
    return (a > b).astype(BF16)


def _sb_prompt_kernel(qi_ref, kj_ref, first_ref, last_ref, q_ref, k_ref, v_ref, o_ref,
                      qs_ref, ls_ref, acc_ref, *, tq):
    s = pl.program_id(2)
    rows = SB_GROUP * tq

    @pl.when(first_ref[s] == 1)
    def _():
        qs = jnp.concatenate([q_ref[:, g * SB_HEAD_DIM:(g + 1) * SB_HEAD_DIM] for g in range(SB_GROUP)], axis=0)
        qs_ref[...] = (qs * SB_QSCALE).astype(BF16)
        ls_ref[...] = jnp.zeros_like(ls_ref)
        acc_ref[...] = jnp.zeros_like(acc_ref)

    kb = k_ref[...].astype(BF16)
    vb = v_ref[...].astype(BF16)
    usel = _suffix_sel()

    def step(valid):
        w, ls = _sb_weights(_dot_nt(qs_ref[...], kb), valid, ls_ref[...], usel, latest_first=False)
        ls_ref[...] = ls
        acc_ref[...] += _dot(w, vb)

    on_diagonal = kj_ref[s] == qi_ref[s]

    @pl.when(on_diagonal)
    def _():
        step(lax.broadcasted_iota(jnp.int32, (rows, tq), 1) < _row_mod((rows, tq), tq))

    @pl.when(jnp.logical_not(on_diagonal))
    def _():
        step(None)

    @pl.when(last_ref[s] == 1)
    def _():
        for g in range(SB_GROUP):
            o_ref[:, g * SB_HEAD_DIM:(g + 1) * SB_HEAD_DIM] = acc_ref[g * tq:(g + 1) * tq]


def _sb_prompt(proj, batch, seq):
    tq = min(seq, 512)
    tk = tq
    tabs = _causal_steps(seq, tq, tk, descending=True)
    nsteps = tabs[0].shape[0]
    nq, nkb = seq // tq, seq // tk
    kcol = SB_WIDTH // SB_HEAD_DIM
    vcol = (SB_WIDTH + SB_KV_WIDTH) // SB_HEAD_DIM
    rows = SB_GROUP * tq
    grid_spec = pltpu.PrefetchScalarGridSpec(
        num_scalar_prefetch=4,
        grid=(batch, SB_KV_HEADS, nsteps),
        in_specs=[
            pl.BlockSpec((tq, SB_GROUP * SB_HEAD_DIM), lambda b, h, s, qi, kj, f, l: (b * nq + qi[s], h)),
            pl.BlockSpec((tk, SB_HEAD_DIM), lambda b, h, s, qi, kj, f, l: (b * nkb + kj[s], kcol + h)),
            pl.BlockSpec((tk, SB_HEAD_DIM), lambda b, h, s, qi, kj, f, l: (b * nkb + kj[s], vcol + h)),
        ],
        out_specs=pl.BlockSpec((tq, SB_GROUP * SB_HEAD_DIM), lambda b, h, s, qi, kj, f, l: (b * nq + qi[s], h)),
        scratch_shapes=[pltpu.VMEM((rows, SB_HEAD_DIM), BF16), pltpu.VMEM((rows, 1), F32),
                        pltpu.VMEM((rows, SB_HEAD_DIM), F32)],
    )
    return pl.pallas_call(
        functools.partial(_sb_prompt_kernel, tq=tq),
        grid_spec=grid_spec,
        out_shape=jax.ShapeDtypeStruct((batch * seq, SB_WIDTH), F32),
        compiler_params=_cparams(("parallel", "parallel", "arbitrary")),
        name="sb_prompt",
    )(*tabs, proj, proj, proj)


def _sb_sample_kernel(pt_ref, q_ref, kn_ref, vn_ref, *refs, t, npp):
    k_refs, v_refs = refs[:npp], refs[npp:2 * npp]
    o_ref, ls_ref, acc_ref = refs[2 * npp:]
    c = pl.program_id(1)
    rows = SB_HEADS * t
    usel = _suffix_sel()
    qs = jnp.concatenate([q_ref[:, h * SB_HEAD_DIM:(h + 1) * SB_HEAD_DIM] for h in range(SB_HEADS)], axis=0)
    qs = (qs * SB_QSCALE).astype(BF16)
    grows = SB_GROUP * t

    def attend(keys, values, valid, ls, acc):
        z = jnp.concatenate([_dot_nt(qs[h * grows:(h + 1) * grows], keys[h])
                             for h in range(SB_KV_HEADS)], axis=0)
        w, ls = _sb_weights(z, valid, ls, usel, latest_first=True)
        pv = jnp.concatenate([_dot(w[h * grows:(h + 1) * grows], values[h])
                              for h in range(SB_KV_HEADS)], axis=0)
        return ls, acc + pv

    @pl.when(c == 0)
    def _():
        pad = jnp.zeros((LANES - t, SB_HEAD_DIM), F32)

        def new_tokens(ref):
            return [jnp.concatenate([ref[:, h * SB_HEAD_DIM:(h + 1) * SB_HEAD_DIM], pad], axis=0).astype(BF16)
                    for h in range(SB_KV_HEADS)]

        tq = _row_mod((rows, LANES), t)
        kidx = lax.broadcasted_iota(jnp.int32, (rows, LANES), 1)
        ls, acc = attend(new_tokens(kn_ref), new_tokens(vn_ref), kidx < tq,
                         jnp.zeros((rows, 1), F32), jnp.zeros((rows, SB_HEAD_DIM), F32))
        ls_ref[...] = ls
        acc_ref[...] = acc

    def pages(page_refs):
        return [jnp.concatenate([r[0, pl.ds(h, LANES, stride=SB_KV_HEADS), :] for r in page_refs],
                                axis=0).astype(BF16) for h in range(SB_KV_HEADS)]

    ls, acc = attend(pages(k_refs), pages(v_refs), None, ls_ref[...], acc_ref[...])
    ls_ref[...] = ls
    acc_ref[...] = acc

    @pl.when(c == pl.num_programs(1) - 1)
    def _():
        for h in range(SB_HEADS):
            o_ref[:, h * SB_HEAD_DIM:(h + 1) * SB_HEAD_DIM] = acc[h * t:(h + 1) * t]


def _pages_per_step(n_pages):
    return max(p for p in range(1, 17) if n_pages % p == 0)


def _sb_sample(proj, cache_k, cache_v, j, page_table, batch, t):
    n_pool, page = cache_k.shape[1], cache_k.shape[2]
    assert page == LANES
    n_pages = page_table.shape[1]
    npp = _pages_per_step(n_pages)
    ck = cache_k.reshape(-1, page * SB_KV_HEADS, SB_HEAD_DIM)
    cv = cache_v.reshape(-1, page * SB_KV_HEADS, SB_HEAD_DIM)
    pt = (page_table + j * n_pool).reshape(-1)
    kcol = SB_WIDTH // SB_KV_WIDTH

    def page_spec(p):
        return pl.BlockSpec((1, page * SB_KV_HEADS, SB_HEAD_DIM),
                            lambda b, c, pt: (pt[b * n_pages + n_pages - 1 - (c * npp + p)], 0, 0))

    grid_spec = pltpu.PrefetchScalarGridSpec(
        num_scalar_prefetch=1,
        grid=(batch, n_pages // npp),
        in_specs=[
            pl.BlockSpec((t, SB_WIDTH), lambda b, c, pt: (b, 0)),
            pl.BlockSpec((t, SB_KV_WIDTH), lambda b, c, pt: (b, kcol)),
            pl.BlockSpec((t, SB_KV_WIDTH), lambda b, c, pt: (b, kcol + 1)),
        ] + [page_spec(p) for p in range(npp)] * 2,
        out_specs=pl.BlockSpec((t, SB_WIDTH), lambda b, c, pt: (b, 0)),
        scratch_shapes=[pltpu.VMEM((SB_HEADS * t, 1), F32), pltpu.VMEM((SB_HEADS * t, SB_HEAD_DIM), F32)],
    )
    return pl.pallas_call(
        functools.partial(_sb_sample_kernel, t=t, npp=npp),
        grid_spec=grid_spec,
        out_shape=jax.ShapeDtypeStruct((batch * t, SB_WIDTH), F32),
        compiler_params=_cparams(("parallel", "arbitrary")),
        name="sb_sample",
    )(pt, proj, proj, proj, *([ck] * npp), *([cv] * npp))


def _rope_tables(pos):
    half = MLA_ROPE // 2
    inv_freq = ROPE_THETA ** (-jnp.arange(half, dtype=F32) / half)
    ang = pos.astype(F32)[:, None] * inv_freq[None, :]
    cos, sin = jnp.cos(ang), jnp.sin(ang)
    zero = jnp.zeros_like(sin)
    reps = LANES // MLA_ROPE
    return (jnp.tile(jnp.concatenate([cos, cos], axis=1), (1, reps)),
            jnp.tile(jnp.concatenate([-sin, zero], axis=1), (1, reps)),
            jnp.tile(jnp.concatenate([zero, sin], axis=1), (1, reps)))


def _rope128(x, c, s_lo, s_hi):
    half = MLA_ROPE // 2
    return x * c + pltpu.roll(x, LANES - half, 1) * s_lo + pltpu.roll(x, half, 1) * s_hi


def _mla_prep_kernel(q_ref, ckv_ref, kr_ref, kvn_ref, wuk_ref, c_ref, slo_ref, shi_ref,
                     qlat_ref, qrope_ref, ckvn_ref, krr_ref):
    c, slo, shi = c_ref[...], slo_ref[...], shi_ref[...]
    for h in range(MLA_HEADS):
        qn = q_ref[:, h * MLA_NOPE:(h + 1) * MLA_NOPE].astype(BF16)
        qlat_ref[:, h * MLA_KV_LORA:(h + 1) * MLA_KV_LORA] = _dot(qn, wuk_ref[h])
    rope0 = MLA_HEADS * MLA_NOPE
    for i in range(MLA_HEADS * MLA_ROPE // LANES):
        qrope_ref[:, i * LANES:(i + 1) * LANES] = _rope128(
            q_ref[:, rope0 + i * LANES:rope0 + (i + 1) * LANES], c, slo, shi)
    ckv = ckv_ref[...]
    ms = jnp.mean(ckv * ckv, axis=-1, keepdims=True)
    ckvn_ref[...] = ckv * lax.rsqrt(ms + NORM_EPS) * kvn_ref[...]
    krr_ref[...] = _rope128(kr_ref[:, :LANES], c, slo, shi)[:, :MLA_ROPE]


def _mla_prep(q, proj, kv_norm, wuk_t, tabs):
    m = q.shape[0]
    tm = min(m, 512)
    qw = MLA_HEADS * (MLA_NOPE + MLA_ROPE)
    row = lambda w: pl.BlockSpec((tm, w), lambda i: (i, 0))
    return pl.pallas_call(
        _mla_prep_kernel,
        grid=(m // tm,),
        in_specs=[
            row(qw),
            pl.BlockSpec((tm, MLA_KV_LORA), lambda i: (i, MLA_Q_LORA // MLA_KV_LORA)),
            pl.BlockSpec((tm, MLA_KV_LORA), lambda i: (i, MLA_Q_LORA // MLA_KV_LORA + 1)),
            pl.BlockSpec((1, MLA_KV_LORA), lambda i: (0, 0)),
            pl.BlockSpec((MLA_HEADS, MLA_NOPE, MLA_KV_LORA), lambda i: (0, 0, 0)),
            row(LANES), row(LANES), row(LANES),
        ],
        out_specs=[row(MLA_HEADS * MLA_KV_LORA), row(MLA_HEADS * MLA_ROPE), row(MLA_KV_LORA), row(MLA_ROPE)],
        out_shape=[
            jax.ShapeDtypeStruct((m, MLA_HEADS * MLA_KV_LORA), F32),
            jax.ShapeDtypeStruct((m, MLA_HEADS * MLA_ROPE), F32),
            jax.ShapeDtypeStruct((m, MLA_KV_LORA), F32),
            jax.ShapeDtypeStruct((m, MLA_ROPE), F32),
        ],
        compiler_params=_cparams(("parallel",)),
        name="mla_prep",
    )(q, proj, proj, kv_norm.reshape(1, MLA_KV_LORA), wuk_t, *tabs)


MLA_QK = MLA_KV_LORA + LANES


def _mla_stack_q(qlat_ref, qrope_ref, t):
    pad = jnp.zeros((t, LANES - MLA_ROPE), F32)
    q = jnp.concatenate(
        [jnp.concatenate([qlat_ref[:, h * MLA_KV_LORA:(h + 1) * MLA_KV_LORA],
                          qrope_ref[:, h * MLA_ROPE:(h + 1) * MLA_ROPE], pad], axis=1)
         for h in range(MLA_HEADS)], axis=0)
    return (q * MLA_QSCALE).astype(BF16)


def _mla_keys(ckv, kr, pad_rows=0):
    k = jnp.concatenate([ckv, kr, jnp.zeros((ckv.shape[0], LANES - MLA_ROPE), F32)], axis=1)
    if pad_rows:
        k = jnp.concatenate([k, jnp.zeros((pad_rows, MLA_QK), F32)], axis=0)
    return k.astype(BF16)


def _softmax_update(s, values, valid, m, l, acc):
    if valid is not None:
        s = jnp.where(valid, s, NEG_BIG)
    m_new = jnp.maximum(m, jnp.max(s, axis=-1, keepdims=True))
    corr = jnp.exp2(m - m_new)
    p = jnp.exp2(s - m_new)
    return (m_new, l * corr + jnp.sum(p, axis=-1, keepdims=True), acc * corr + _dot(p.astype(BF16), values))


def _mla_update(q, kcat, valid, m, l, acc):
    return _softmax_update(_dot_nt(q, kcat), kcat[:, :MLA_KV_LORA], valid, m, l, acc)


def _mla_finish(acc, l, wuv_ref, o_ref, t):
    o_lat = (acc / l).astype(BF16)
    for h in range(MLA_HEADS):
        o_ref[:, h * MLA_V:(h + 1) * MLA_V] = _dot(o_lat[h * t:(h + 1) * t], wuv_ref[h])


def _mla_prompt_kernel(qi_ref, kj_ref, first_ref, last_ref, qlat_ref, qrope_ref, ckv_ref, kr_ref, wuv_ref,
                       o_ref, q_ref, m_ref, l_ref, acc_ref, *, tq, tk):
    s = pl.program_id(1)
    rows = MLA_HEADS * tq

    @pl.when(first_ref[s] == 1)
    def _():
        q_ref[...] = _mla_stack_q(qlat_ref, qrope_ref, tq)
        m_ref[...] = jnp.full_like(m_ref, NEG_BIG)
        l_ref[...] = jnp.zeros_like(l_ref)
        acc_ref[...] = jnp.zeros_like(acc_ref)

    kcat = _mla_keys(ckv_ref[...], kr_ref[...])

    def step(valid):
        m, l, acc = _mla_update(q_ref[...], kcat, valid, m_ref[...], l_ref[...], acc_ref[...])
        m_ref[...] = m
        l_ref[...] = l
        acc_ref[...] = acc

    crosses = (kj_ref[s] + 1) * tk - 1 > qi_ref[s] * tq

    @pl.when(crosses)
    def _():
        qpos = qi_ref[s] * tq + _row_mod((rows, tk), tq)
        kpos = kj_ref[s] * tk + lax.broadcasted_iota(jnp.int32, (rows, tk), 1)
        step(kpos <= qpos)

    @pl.when(jnp.logical_not(crosses))
    def _():
        step(None)

    @pl.when(last_ref[s] == 1)
    def _():
        _mla_finish(acc_ref[...], l_ref[...], wuv_ref, o_ref, tq)


def _mla_prompt(qlat, qrope, ckvn, krr, wuv_t, batch, seq):
    tq = min(seq, 64)
    tk = min(seq, 512)
    tabs = _causal_steps(seq, tq, tk, descending=False)
    nsteps = tabs[0].shape[0]
    nq, nkb = seq // tq, seq // tk
    rows = MLA_HEADS * tq
    grid_spec = pltpu.PrefetchScalarGridSpec(
        num_scalar_prefetch=4,
        grid=(batch, nsteps),
        in_specs=[
            pl.BlockSpec((tq, MLA_HEADS * MLA_KV_LORA), lambda b, s, qi, kj, f, l: (b * nq + qi[s], 0)),
            pl.BlockSpec((tq, MLA_HEADS * MLA_ROPE), lambda b, s, qi, kj, f, l: (b * nq + qi[s], 0)),
            pl.BlockSpec((tk, MLA_KV_LORA), lambda b, s, qi, kj, f, l: (b * nkb + kj[s], 0)),
            pl.BlockSpec((tk, MLA_ROPE), lambda b, s, qi, kj, f, l: (b * nkb + kj[s], 0)),
            pl.BlockSpec((MLA_HEADS, MLA_KV_LORA, MLA_V), lambda b, s, qi, kj, f, l: (0, 0, 0)),
        ],
        out_specs=pl.BlockSpec((tq, MLA_WIDTH), lambda b, s, qi, kj, f, l: (b * nq + qi[s], 0)),
        scratch_shapes=[pltpu.VMEM((rows, MLA_QK), BF16),
                        pltpu.VMEM((rows, 1), F32), pltpu.VMEM((rows, 1), F32),
                        pltpu.VMEM((rows, MLA_KV_LORA), F32)],
    )
    return pl.pallas_call(
        functools.partial(_mla_prompt_kernel, tq=tq, tk=tk),
        grid_spec=grid_spec,
        out_shape=jax.ShapeDtypeStruct((batch * seq, MLA_WIDTH), F32),
        compiler_params=_cparams(("parallel", "arbitrary")),
        name="mla_prompt",
    )(*tabs, qlat, qrope, ckvn, krr, wuv_t)


def _mla_sample_kernel(pt_ref, qlat_ref, qrope_ref, cn_ref, kn_ref, wuv_ref, *refs, t, npp):
    c_refs, k_refs = refs[:npp], refs[npp:2 * npp]
    o_ref, m_ref, l_ref, acc_ref = refs[2 * npp:]
    c = pl.program_id(1)
    rows = MLA_HEADS * t
    q = _mla_stack_q(qlat_ref, qrope_ref, t)

    @pl.when(c == 0)
    def _():
        tq = _row_mod((rows, LANES), t)
        kidx = lax.broadcasted_iota(jnp.int32, (rows, LANES), 1)
        m, l, acc = _mla_update(q, _mla_keys(cn_ref[...], kn_ref[...], LANES - t), kidx <= tq,
                                jnp.full((rows, 1), NEG_BIG, F32), jnp.zeros((rows, 1), F32),
                                jnp.zeros((rows, MLA_KV_LORA), F32))
        m_ref[...] = m
        l_ref[...] = l
        acc_ref[...] = acc

    ckv = jnp.concatenate([r[0] for r in c_refs], axis=0).astype(BF16)
    kr_t = jnp.concatenate([r[0] for r in k_refs], axis=1)
    kr_t = jnp.concatenate([kr_t, jnp.zeros((LANES - MLA_ROPE, kr_t.shape[1]), F32)], axis=0).astype(BF16)
    s = _dot_nt(q[:, :MLA_KV_LORA], ckv) + _dot(q[:, MLA_KV_LORA:], kr_t)
    m, l, acc = _softmax_update(s, ckv, None, m_ref[...], l_ref[...], acc_ref[...])
    m_ref[...] = m
    l_ref[...] = l
    acc_ref[...] = acc

    @pl.when(c == pl.num_programs(1) - 1)
    def _():
        _mla_finish(acc, l, wuv_ref, o_ref, t)


def _mla_sample(qlat, qrope, ckvn, krr, wuv_t, cache_ckv, cache_kr, j, page_table, batch, t):
    n_pool, page = cache_ckv.shape[1], cache_ckv.shape[2]
    assert page == LANES
    n_pages = page_table.shape[1]
    npp = _pages_per_step(n_pages)
    cc = cache_ckv.reshape(-1, page, MLA_KV_LORA)
    ck = jnp.swapaxes(cache_kr, 2, 3).reshape(-1, MLA_ROPE, page)
    pt = (page_table + j * n_pool).reshape(-1)
    rows = MLA_HEADS * t

    def page_spec(p, shape):
        return pl.BlockSpec((1,) + shape, lambda b, c, pt: (pt[b * n_pages + c * npp + p], 0, 0))

    grid_spec = pltpu.PrefetchScalarGridSpec(
        num_scalar_prefetch=1,
        grid=(batch, n_pages // npp),
        in_specs=[
            pl.BlockSpec((t, MLA_HEADS * MLA_KV_LORA), lambda b, c, pt: (b, 0)),
            pl.BlockSpec((t, MLA_HEADS * MLA_ROPE), lambda b, c, pt: (b, 0)),
            pl.BlockSpec((t, MLA_KV_LORA), lambda b, c, pt: (b, 0)),
            pl.BlockSpec((t, MLA_ROPE), lambda b, c, pt: (b, 0)),
            pl.BlockSpec((MLA_HEADS, MLA_KV_LORA, MLA_V), lambda b, c, pt: (0, 0, 0)),
        ] + [page_spec(p, (page, MLA_KV_LORA)) for p in range(npp)]
          + [page_spec(p, (MLA_ROPE, page)) for p in range(npp)],
        out_specs=pl.BlockSpec((t, MLA_WIDTH), lambda b, c, pt: (b, 0)),
        scratch_shapes=[pltpu.VMEM((rows, 1), F32), pltpu.VMEM((rows, 1), F32),
                        pltpu.VMEM((rows, MLA_KV_LORA), F32)],
    )
    return pl.pallas_call(
        functools.partial(_mla_sample_kernel, t=t, npp=npp),
        grid_spec=grid_spec,
        out_shape=jax.ShapeDtypeStruct((batch * t, MLA_WIDTH), F32),
        compiler_params=_cparams(("parallel", "arbitrary")),
        name="mla_sample",
    )(pt, qlat, qrope, ckvn, krr, wuv_t, *([cc] * npp), *([ck] * npp))


def kernel(x_prompt, x_sample, state_ssd_conv, state_ssd_h, cache_sb_k, cache_sb_v, cache_mla_ckv,
           cache_mla_krope, page_table, norm_pre, norm_post, ssd_w_in, ssd_conv_w, ssd_conv_b, ssd_dt_bias,
           ssd_a_log, ssd_d, ssd_norm_w, ssd_w_out, sb_w_in, sb_w_out, mla_w_in, mla_q_norm, mla_kv_norm,
           mla_w_uq, mla_w_uk, mla_w_uv, mla_w_out):
    bp, seq, d = x_prompt.shape
    bs, t, _ = x_sample.shape
    depth = norm_pre.shape[0]
    past = page_table.shape[1] * cache_sb_k.shape[2]
    assert seq % SSD_CHUNK == 0 and t >= SSD_CONV - 1
    xp = x_prompt.reshape(bp * seq, d)
    xs = x_sample.reshape(bs * t, d)
    outs = {k: [] for k in ("p_conv", "p_h", "p_k", "p_v", "p_ckv", "p_kr",
                            "s_conv", "s_h", "s_k", "s_v", "s_ckv", "s_kr")}
    zx = SSD_D_INNER + SSD_CONV_DIM
    for i in range(depth):
        kind, j = i % N_MIXERS, i // N_MIXERS
        if kind == 0:
            w_zx = ssd_w_in[j][:, :zx].astype(BF16)
            w_dt = jnp.pad(ssd_w_in[j][:, zx:], ((0, 0), (0, LANES - SSD_HEADS))).astype(BF16)
            w_out = ssd_w_out[j].astype(BF16)
            consts = _ssd_consts(ssd_conv_w[j], ssd_conv_b[j], ssd_dt_bias[j], ssd_a_log[j], ssd_d[j],
                                 ssd_norm_w[j])
            proj_p = _norm_matmul(xp, norm_pre[i], w_zx, name="ssd_in_p")
            proj_s = _norm_matmul(xs, norm_pre[i], w_zx, name="ssd_in_s")
            dt_p = _norm_matmul(xp, norm_pre[i], w_dt, name="ssd_dt_p")
            dt_s = _norm_matmul(xs, norm_pre[i], w_dt, name="ssd_dt_s")
            y_p, h_p = _ssd_prompt(proj_p, dt_p, consts, bp, seq)
            h0 = state_ssd_h.reshape(-1, SSD_D_INNER, SSD_STATE)
            y_s, h_s = _ssd_sample(proj_s, dt_s, state_ssd_conv[j], h0, j, consts, bs, t)
            outs["p_conv"].append(proj_p.reshape(bp, seq, zx)[:, seq - (SSD_CONV - 1):, SSD_D_INNER:])
            outs["s_conv"].append(proj_s.reshape(bs, t, zx)[:, t - (SSD_CONV - 1):, SSD_D_INNER:])
            outs["p_h"].append(h_p.reshape(bp, SSD_HEADS, SSD_HEAD_DIM, SSD_STATE))
            outs["s_h"].append(h_s.reshape(bs, SSD_HEADS, SSD_HEAD_DIM, SSD_STATE))
            xp = _matmul_post(y_p, w_out, xp, norm_post[i], name="ssd_out_p")
            xs = _matmul_post(y_s, w_out, xs, norm_post[i], name="ssd_out_s")
        elif kind == 1:
            w_in = sb_w_in[j].astype(BF16)
            w_out = sb_w_out[j].astype(BF16)
            proj_p = _norm_matmul(xp, norm_pre[i], w_in, name="sb_in_p")
            proj_s = _norm_matmul(xs, norm_pre[i], w_in, name="sb_in_s")
            gcol = SB_WIDTH + 2 * SB_KV_WIDTH
            a_p = _sb_prompt(proj_p, bp, seq)
            a_s = _sb_sample(proj_s, cache_sb_k, cache_sb_v, j, page_table, bs, t)
            for tag, pr, nb, ln in (("p", proj_p, bp, seq), ("s", proj_s, bs, t)):
                kv = pr.reshape(nb, ln, -1)
                outs[tag + "_k"].append(kv[:, :, SB_WIDTH:SB_WIDTH + SB_KV_WIDTH].reshape(nb, ln, SB_KV_HEADS, SB_HEAD_DIM))
                outs[tag + "_v"].append(kv[:, :, SB_WIDTH + SB_KV_WIDTH:gcol].reshape(nb, ln, SB_KV_HEADS, SB_HEAD_DIM))
            xp = _matmul_post(a_p, w_out, xp, norm_post[i], gate=proj_p, gate_col=gcol, name="sb_out_p")
            xs = _matmul_post(a_s, w_out, xs, norm_post[i], gate=proj_s, gate_col=gcol, name="sb_out_s")
        else:
            o1, o2, o3 = MLA_Q_LORA, MLA_Q_LORA + MLA_KV_LORA, MLA_Q_LORA + MLA_KV_LORA + MLA_ROPE
            w = mla_w_in[j]
            w_in = jnp.concatenate([w[:, :o3], jnp.zeros((d, MLA_KV_LORA - MLA_ROPE), F32), w[:, o3:]],
                                   axis=1).astype(BF16)
            gcol = o2 + MLA_KV_LORA
            uq = mla_w_uq[j]
            w_uq = jnp.concatenate([uq[:, :, :MLA_NOPE].reshape(MLA_Q_LORA, -1),
                                    uq[:, :, MLA_NOPE:].reshape(MLA_Q_LORA, -1)], axis=1).astype(BF16)
            wuk_t = jnp.transpose(mla_w_uk[j], (1, 2, 0)).astype(BF16)
            wuv_t = jnp.transpose(mla_w_uv[j], (1, 0, 2)).astype(BF16)
            w_out = mla_w_out[j].astype(BF16)
            res = []
            for tag, xx, nb, ln, pos in (("p", xp, bp, seq, jnp.arange(seq)), ("s", xs, bs, t, past + jnp.arange(t))):
                proj = _norm_matmul(xx, norm_pre[i], w_in, name="mla_in_" + tag)
                q = _norm_matmul(proj, mla_q_norm[j], w_uq, name="mla_uq_" + tag)
                tabs = tuple(jnp.tile(tb, (nb, 1)) for tb in _rope_tables(pos))
                qlat, qrope, ckvn, krr = _mla_prep(q, proj, mla_kv_norm[j], wuk_t, tabs)
                outs[tag + "_ckv"].append(ckvn.reshape(nb, ln, MLA_KV_LORA))
                outs[tag + "_kr"].append(krr.reshape(nb, ln, MLA_ROPE))
                if tag == "p":
                    o = _mla_prompt(qlat, qrope, ckvn, krr, wuv_t, nb, ln)
                else:
                    o = _mla_sample(qlat, qrope, ckvn, krr, wuv_t, cache_mla_ckv, cache_mla_krope, j,
                                    page_table, nb, ln)
                res.append(_matmul_post(o, w_out, xx, norm_post[i], gate=proj, gate_col=gcol,
                                        name="mla_out_" + tag))
            xp, xs = res
    st = lambda k: jnp.stack(outs[k])
    return (xp.reshape(bp, seq, d), xs.reshape(bs, t, d),
            st("p_conv"), st("p_h"), st("p_k"), st("p_v"), st("p_ckv"), st("p_kr"),
            st("s_conv"), st("s_h"), st("s_k"), st("s_v"), st("s_ckv"), st("s_kr"))
```
